```python
import math
import jax, jax.numpy as jnp
from jax import lax
import numpy as np

D_MODEL = 2048
BATCH = 4
SEQ = 2048
DEPTH = 2
DEC_BATCH = 128
DEC_SEQ = 1
PAST_LEN = 16384
PAGE_SIZE = 128

HEAD_DIM = 128
N_MIX_HEADS = D_MODEL // HEAD_DIM
MOBA_HEADS = N_MIX_HEADS // 4
FOX_HEADS = N_MIX_HEADS // 4
MLA_HEADS = N_MIX_HEADS // 2
MOBA_BLOCK = 256
MOBA_TOPK = 3
MOBA_QCHUNK = 32
QUERY_BLOCK = 128
MLA_Q_LORA = 512
MLA_KV_LORA = 128
MLA_NOPE = 128
MLA_ROPE = 64
MLA_V = 128
ROPE_THETA = 10000.0
N_BUCKETS = 32
MAX_DISTANCE = 128
MEM_LEN = 256
X_HEADS = 4
FOX_GATE_BIAS = 3.0
MIX_WIDTH = MOBA_HEADS * HEAD_DIM + FOX_HEADS * HEAD_DIM + MLA_HEADS * MLA_V
D_FF = -(-8 * D_MODEL // (3 * 256)) * 256
IN_SPLITS = (MOBA_HEADS * HEAD_DIM, HEAD_DIM, HEAD_DIM,
             FOX_HEADS * HEAD_DIM, HEAD_DIM, HEAD_DIM, FOX_HEADS,
             MLA_Q_LORA, MLA_KV_LORA, MLA_ROPE)
IN_COLS = sum(IN_SPLITS)

kernel_name = "hybrid_moba_fox_mla_decoder_step"


def rmsnorm(x, g, eps=1e-6):
    x32 = x.astype(jnp.float32)
    y = x32 * lax.rsqrt(jnp.mean(x32 * x32, axis=-1, keepdims=True) + eps)
    return y.astype(x.dtype) * g


def rope(x, pos):
    half = x.shape[-1] // 2
    inv = ROPE_THETA ** (-jnp.arange(half, dtype=jnp.float32) / half)
    ang = pos.astype(jnp.float32)[:, None] * inv[None, :]
    cos = jnp.cos(ang)[:, None, :].astype(x.dtype)
    sin = jnp.sin(ang)[:, None, :].astype(x.dtype)
    x1, x2 = x[..., :half], x[..., half:]
    return jnp.concatenate([x1 * cos - x2 * sin, x2 * cos + x1 * sin], axis=-1)


def t5_bucket(dist):
    n = jnp.maximum(dist, 0)
    max_exact = N_BUCKETS // 2
    large = max_exact + (jnp.log(jnp.maximum(n, 1).astype(jnp.float32) / max_exact)
                         / math.log(MAX_DISTANCE / max_exact) * (N_BUCKETS - max_exact)).astype(jnp.int32)
    return jnp.where(n < max_exact, n, jnp.minimum(large, N_BUCKETS - 1))


def sweep_queries(fn, qblock, qpos, *qs):
    tq = qpos.shape[0]
    if tq <= qblock or tq % qblock:
        return fn(qpos, *qs)
    n = tq // qblock
    split = lambda a: jnp.moveaxis(a.reshape(a.shape[0], n, qblock, *a.shape[2:]), 1, 0)
    out = lax.map(lambda args: fn(*args), (qpos.reshape(n, qblock),) + tuple(split(a) for a in qs))
    out = jnp.moveaxis(out, 0, 1)
    return out.reshape(out.shape[0], tq, *out.shape[3:])


def moba_attention(q, qpos, k, v, rel_bias):
    b, L, d = k.shape
    nb = max(-(-L // MOBA_BLOCK), MOBA_TOPK)
    pad = ((0, 0), (0, nb * MOBA_BLOCK - L), (0, 0))
    kb = jnp.pad(k, pad).reshape(b, nb, MOBA_BLOCK, d)
    vb = jnp.pad(v, pad).reshape(b, nb, MOBA_BLOCK, d)
    kmean = jnp.mean(kb.astype(jnp.float32), axis=2)
    bias_tab = rel_bias.T
    hidx = jnp.arange(q.shape[2])[None, None, :, None, None]
    offs = jnp.arange(MOBA_BLOCK)
    scale = HEAD_DIM ** -0.5
    take = jax.vmap(lambda blocks, idx: blocks[idx])

    def block_fn(qp, qc):
        own = qp // MOBA_BLOCK
        gate = jnp.einsum("bqhd,bnd->bqhn", qc.astype(jnp.float32), kmean)
        fully_past = jnp.arange(nb)[None, :] < own[:, None]
        gate = jnp.where(fully_past[None, :, None, :], gate, -jnp.inf)
        _, sel = lax.top_k(gate, MOBA_TOPK)
        valid = jnp.arange(MOBA_TOPK)[None, :] < own[:, None]
        ks, vs = take(kb, sel), take(vb, sel)
        ko, vo = kb[:, own], vb[:, own]
        dist_sel = qp[None, :, None, None, None] - (sel[..., None] * MOBA_BLOCK + offs)
        s_sel = (jnp.einsum("bqhd,bqhnkd->bqhnk", qc, ks).astype(jnp.float32) * scale
                 + bias_tab[hidx, t5_bucket(dist_sel)])
        s_sel = jnp.where(valid[None, :, None, :, None], s_sel, -jnp.inf)
        dist_own = qp[:, None] - (own[:, None] * MOBA_BLOCK + offs)
        bias_own = jnp.moveaxis(rel_bias[t5_bucket(dist_own)], -1, 1)
        s_own = jnp.einsum("bqhd,bqkd->bqhk", qc, ko).astype(jnp.float32) * scale + bias_own
        s_own = jnp.where((dist_own >= 0)[None, :, None, :], s_own, -jnp.inf)
        bq, tq, h = s_own.shape[:3]
        pr = jax.nn.softmax(jnp.concatenate([s_sel.reshape(bq, tq, h, -1), s_own], axis=-1), axis=-1)
        pr = pr.astype(v.dtype)
        p_sel = pr[..., :MOBA_TOPK * MOBA_BLOCK].reshape(s_sel.shape)
        p_own = pr[..., MOBA_TOPK * MOBA_BLOCK:]
        return (jnp.einsum("bqhnk,bqhnkd->bqhd", p_sel, vs)
                + jnp.einsum("bqhk,bqkd->bqhd", p_own, vo))

    return sweep_queries(block_fn, MOBA_QCHUNK, qpos, q)


def fox_attention(q, qpos, k, v, cum_q, cum_k):
    L = k.shape[1]
    kidx = jnp.arange(L)
    ck = jnp.moveaxis(cum_k, 2, 1)[:, :, None, :]
    scale = HEAD_DIM ** -0.5

    def block_fn(qp, qc, cq):
        s = jnp.einsum("bqhd,bkd->bhqk", qc, k).astype(jnp.float32) * scale
        s = s + (jnp.moveaxis(cq, 2, 1)[..., None] - ck)
        s = jnp.where(kidx[None, :] <= qp[:, None], s, -jnp.inf)
        pr = jax.nn.softmax(s, axis=-1).astype(v.dtype)
        return jnp.einsum("bhqk,bkd->bqhd", pr, v)

    return sweep_queries(block_fn, QUERY_BLOCK, qpos, q, cum_q)


def mla_attention(q_lat, q_rope, qpos, c, kr):
    L = c.shape[1]
    kidx = jnp.arange(L)
    scale = (MLA_NOPE + MLA_ROPE) ** -0.5

    def block_fn(qp, ql, qr):
        s = (jnp.einsum("bqhr,bkr->bhqk", ql, c)
             + jnp.einsum("bqhe,bke->bhqk", qr, kr)).astype(jnp.float32) * scale
        s = jnp.where(kidx[None, :] <= qp[:, None], s, -jnp.inf)
        pr = jax.nn.softmax(s, axis=-1).astype(c.dtype)
        return jnp.einsum("bhqk,bkr->bqhr", pr, c)

    return sweep_queries(block_fn, QUERY_BLOCK, qpos, q_lat, q_rope)


def memory_kv(mem, g_mem, w_xk, w_xv, g_xk):
    b, m, _ = mem.shape
    u = rmsnorm(mem, g_mem)
    k = rmsnorm((u @ w_xk).reshape(b, m, X_HEADS, HEAD_DIM), g_xk)
    v = (u @ w_xv).reshape(b, m, X_HEADS, HEAD_DIM)
    return k, v


def hybrid_layer(h, pos, past, mem_kv, rel_bias, p):
    b, t, _ = h.shape
    heads = lambda a, n: a.reshape(b, t, n, -1)
    u = rmsnorm(h, p["g_mix"])
    z = u @ p["w_in"]
    mq, mk, mv, fq, fk, fv, fz, cqa, ckv, kr = jnp.split(z, np.cumsum(IN_SPLITS)[:-1].tolist(), axis=-1)
    mq = rmsnorm(heads(mq, MOBA_HEADS), p["g_moba_q"])
    mk = rmsnorm(mk, p["g_moba_k"])
    fq = rmsnorm(heads(fq, FOX_HEADS), p["g_fox_q"])
    fk = rmsnorm(fk, p["g_fox_k"])
    logf = jax.nn.log_sigmoid(fz.astype(jnp.float32) + p["b_fox_f"].astype(jnp.float32))
    q = rmsnorm(heads(rmsnorm(cqa, p["g_mla_qa"]) @ p["w_mla_uq"], MLA_HEADS), p["g_mla_q"])
    q_nope, q_rope = q[..., :MLA_NOPE], rope(q[..., MLA_NOPE:], pos)
    ckv = rmsnorm(ckv, p["g_mla_kv"])
    kr = rope(rmsnorm(kr, p["g_mla_kr"])[:, :, None, :], pos)[:, :, 0, :]
    q_lat = jnp.einsum("bqhn,rhn->bqhr", q_nope, p["w_mla_uk"])

    new_rows = (mk, mv, fk, fv, logf, ckv, kr)
    if past is None:
        full = new_rows
    else:
        full = tuple(jnp.concatenate([old.astype(new.dtype), new], axis=1)
                     for old, new in zip(past, new_rows))
    k_moba, v_moba, k_fox, v_fox, logf_all, c_all, kr_all = full
    cum = jnp.cumsum(logf_all.astype(jnp.float32), axis=1)

    o_moba = moba_attention(mq, pos, k_moba, v_moba, rel_bias)
    o_fox = fox_attention(fq, pos, k_fox, v_fox, cum[:, -t:], cum)
    o_mla = jnp.einsum("bqhr,rhv->bqhv", mla_attention(q_lat, q_rope, pos, c_all, kr_all), p["w_mla_uv"])
    o = jnp.concatenate([o_moba.reshape(b, t, -1), o_fox.reshape(b, t, -1), o_mla.reshape(b, t, -1)], axis=-1)
    h = h + o @ p["w_o"]

    mem_k, mem_v = mem_kv
    u = rmsnorm(h, p["g_x"])
    xq = rmsnorm(heads(u @ p["w_xq"], X_HEADS), p["g_xq"])
    s = jnp.einsum("bqhd,bmhd->bhqm", xq, mem_k).astype(jnp.float32) * HEAD_DIM ** -0.5
    pr = jax.nn.softmax(s, axis=-1).astype(mem_v.dtype)
    h = h + jnp.einsum("bhqm,bmhd->bqhd", pr, mem_v).reshape(b, t, -1) @ p["w_xo"]

    u = rmsnorm(h, p["g_ffn"])
    h = h + (jax.nn.silu(u @ p["w_gate"]) * (u @ p["w_up"])) @ p["w_down"]
    return h, new_rows


def setup_inputs(seed: int = 0) -> dict:
    key = jax.random.key(seed)
    keys = iter(jax.random.split(key, 64))
    f32 = jnp.float32

    def normal(shape, scale=1.0):
        return scale * jax.random.normal(next(keys), shape, f32)

    def gain(shape):
        return 1.0 + 0.02 * normal(shape)

    n_pages = PAST_LEN // PAGE_SIZE
    n_used = DEC_BATCH * n_pages
    n_pool = n_used + max(1, n_used // 4)
    page_table = jax.random.permutation(next(keys), n_pool)[:n_used].reshape(DEC_BATCH, n_pages).astype(jnp.int32)
    pool = (DEPTH, n_pool, PAGE_SIZE)
    L = (DEPTH,)
    return {
        "x_prompt": normal((BATCH, SEQ, D_MODEL)),
        "x_sample": normal((DEC_BATCH, DEC_SEQ, D_MODEL)),
        "cache_moba_k": normal(pool + (HEAD_DIM,)),
        "cache_moba_v": normal(pool + (HEAD_DIM,)),
        "cache_fox_k": normal(pool + (HEAD_DIM,)),
        "cache_fox_v": normal(pool + (HEAD_DIM,)),
        "cache_fox_logf": jax.nn.log_sigmoid(FOX_GATE_BIAS + normal(pool + (FOX_HEADS,))),
        "cache_mla_ckv": normal(pool + (MLA_KV_LORA,)),
        "cache_mla_krope": normal(pool + (MLA_ROPE,)),
        "cache_mem_k": normal((DEPTH, DEC_BATCH, MEM_LEN, X_HEADS, HEAD_DIM)),
        "cache_mem_v": normal((DEPTH, DEC_BATCH, MEM_LEN, X_HEADS, HEAD_DIM)),
        "page_table": page_table,
        "mem_prompt": normal((BATCH, MEM_LEN, D_MODEL)),
        "rel_bias": normal((N_BUCKETS, MOBA_HEADS), 0.2),
        "g_mix": gain(L + (D_MODEL,)),
        "w_in": normal(L + (D_MODEL, IN_COLS), D_MODEL ** -0.5),
        "b_fox_f": FOX_GATE_BIAS + normal(L + (FOX_HEADS,), 0.1),
        "g_moba_q": gain(L + (HEAD_DIM,)),
        "g_moba_k": gain(L + (HEAD_DIM,)),
        "g_fox_q": gain(L + (HEAD_DIM,)),
        "g_fox_k": gain(L + (HEAD_DIM,)),
        "g_mla_qa": gain(L + (MLA_Q_LORA,)),
        "w_mla_uq": normal(L + (MLA_Q_LORA, MLA_HEADS * (MLA_NOPE + MLA_ROPE)), MLA_Q_LORA ** -0.5),
        "g_mla_q": gain(L + (MLA_NOPE + MLA_ROPE,)),
        "g_mla_kv": gain(L + (MLA_KV_LORA,)),
        "g_mla_kr": gain(L + (MLA_ROPE,)),
        "w_mla_uk": normal(L + (MLA_KV_LORA, MLA_HEADS, MLA_NOPE), MLA_KV_LORA ** -0.5),
        "w_mla_uv": normal(L + (MLA_KV_LORA, MLA_HEADS, MLA_V), MLA_KV_LORA ** -0.5),
        "w_o": normal(L + (MIX_WIDTH, D_MODEL), MIX_WIDTH ** -0.5),
        "g_x": gain(L + (D_MODEL,)),
        "g_mem": gain(L + (D_MODEL,)),
        "w_xq": normal(L + (D_MODEL, X_HEADS * HEAD_DIM), D_MODEL ** -0.5),
        "w_xk": normal(L + (D_MODEL, X_HEADS * HEAD_DIM), D_MODEL ** -0.5),
        "w_xv": normal(L + (D_MODEL, X_HEADS * HEAD_DIM), D_MODEL ** -0.5),
        "g_xq": gain(L + (HEAD_DIM,)),
        "g_xk": gain(L + (HEAD_DIM,)),
        "w_xo": normal(L + (X_HEADS * HEAD_DIM, D_MODEL), (X_HEADS * HEAD_DIM) ** -0.5),
        "g_ffn": gain(L + (D_MODEL,)),
        "w_gate": normal(L + (D_MODEL, D_FF), D_MODEL ** -0.5),
        "w_up": normal(L + (D_MODEL, D_FF), D_MODEL ** -0.5),
        "w_down": normal(L + (D_FF, D_MODEL), D_FF ** -0.5),
    }


def reference(x_prompt, x_sample, cache_moba_k, cache_moba_v, cache_fox_k, cache_fox_v,
              cache_fox_logf, cache_mla_ckv, cache_mla_krope, cache_mem_k, cache_mem_v,
              page_table, mem_prompt, rel_bias, g_mix, w_in, b_fox_f, g_moba_q, g_moba_k,
              g_fox_q, g_fox_k, g_mla_qa, w_mla_uq, g_mla_q, g_mla_kv, g_mla_kr, w_mla_uk,
              w_mla_uv, w_o, g_x, g_mem, w_xq, w_xk, w_xv, g_xq, g_xk, w_xo, g_ffn,
              w_gate, w_up, w_down):
    n_seq, n_pages = page_table.shape

    def gather_past(cache, l):
        rows = cache[l, page_table]
        return rows.reshape(n_seq, n_pages * cache.shape[2], *cache.shape[3:])

    pos_p = jnp.arange(x_prompt.shape[1], dtype=jnp.int32)
    pos_s = PAST_LEN + jnp.arange(x_sample.shape[1], dtype=jnp.int32)
    paged = (cache_moba_k, cache_moba_v, cache_fox_k, cache_fox_v,
             cache_fox_logf, cache_mla_ckv, cache_mla_krope)
    hp, hs = x_prompt, x_sample
    rows_p, rows_s, mem_k_rows, mem_v_rows = [], [], [], []
    for l in range(DEPTH):
        lp = {"g_mix": g_mix[l], "w_in": w_in[l], "b_fox_f": b_fox_f[l],
              "g_moba_q": g_moba_q[l], "g_moba_k": g_moba_k[l],
              "g_fox_q": g_fox_q[l], "g_fox_k": g_fox_k[l],
              "g_mla_qa": g_mla_qa[l], "w_mla_uq": w_mla_uq[l], "g_mla_q": g_mla_q[l],
              "g_mla_kv": g_mla_kv[l], "g_mla_kr": g_mla_kr[l],
              "w_mla_uk": w_mla_uk[l], "w_mla_uv": w_mla_uv[l], "w_o": w_o[l],
              "g_x": g_x[l], "w_xq": w_xq[l], "g_xq": g_xq[l], "w_xo": w_xo[l],
              "g_ffn": g_ffn[l], "w_gate": w_gate[l], "w_up": w_up[l], "w_down": w_down[l]}
        mem_kv_p = memory_kv(mem_prompt, g_mem[l], w_xk[l], w_xv[l], g_xk[l])
        hp, new_p = hybrid_layer(hp, pos_p, None, mem_kv_p, rel_bias, lp)
        past = tuple(gather_past(c, l) for c in paged)
        hs, new_s = hybrid_layer(hs, pos_s, past, (cache_mem_k[l], cache_mem_v[l]), rel_bias, lp)
        rows_p.append(new_p)
        rows_s.append(new_s)
        mem_k_rows.append(mem_kv_p[0])
        mem_v_rows.append(mem_kv_p[1])

    stk = lambda rows, i: jnp.stack([r[i] for r in rows])
    return (hp, hs,
            stk(rows_p, 0), stk(rows_p, 1), stk(rows_p, 2), stk(rows_p, 3),
            stk(rows_p, 4), stk(rows_p, 5), stk(rows_p, 6),
            jnp.stack(mem_k_rows), jnp.stack(mem_v_rows),
            stk(rows_s, 0), stk(rows_s, 1), stk(rows_s, 2), stk(rows_s, 3),
            stk(rows_s, 4), stk(rows_s, 5), stk(rows_s, 6))
```

```python
import functools
import math

import numpy as np
import jax
import jax.numpy as jnp
from jax import lax
from jax.experimental import pallas as pl
from jax.experimental.pallas import tpu as pltpu

F32 = jnp.float32
BF16 = jnp.bfloat16
NEG_INF = float("-inf")

EPS = 1e-6
HEAD_DIM = 128
LANES = 128
MOBA_HEADS = 4
FOX_HEADS = 4
MLA_HEADS = 8
X_HEADS = 4
MOBA_BLOCK = 256
MOBA_TOPK = 3
MLA_Q_LORA = 512
MLA_KV_LORA = 128
MLA_NOPE = 128
MLA_ROPE = 64
ROPE_THETA = 10000.0
N_BUCKETS = 32
MAX_DISTANCE = 128
PAGE_SIZE = 128
VMEM_LIMIT = 56 * 1024 * 1024

C_MQ, C_MK, C_MV = 0, 512, 640
C_FQ, C_FK, C_FV = 768, 1280, 1408
C_QA, C_CKV, C_KR = 1536, 2048, 2176
C_FZ = C_KR + MLA_ROPE
IN_PAD = 2304

_NT = (((1,), (1,)), ((), ()))


def _cparams(sem, vmem=VMEM_LIMIT):
    return pltpu.CompilerParams(dimension_semantics=sem, vmem_limit_bytes=vmem)


def _pick_tile(n, target, mult=16):
    best = None
    for t in range(mult, min(n, target) + 1, mult):
        if n % t == 0:
            best = t
    assert best is not None, (n, target, mult)
    return best


def _rms(x, g):
    ms = jnp.mean(x * x, axis=-1, keepdims=True)
    return x * lax.rsqrt(ms + EPS) * g


def _dot(a, b):
    return jnp.dot(a, b, preferred_element_type=F32)


def _dot_nt(a, b, precision=None):
    return lax.dot_general(a, b, _NT, precision=precision, preferred_element_type=F32)


def _norm_matmul_kernel(x_ref, g_ref, w_ref, o_ref, xn_ref):
    @pl.when(pl.program_id(1) == 0)
    def _():
        xn_ref[...] = _rms(x_ref[...], g_ref[...]).astype(BF16)

    o_ref[...] = _dot(xn_ref[...], w_ref[...])


def norm_matmul(x, g, w, *, tm_target=640, tn_target=768):
    m, k = x.shape
    n = w.shape[1]
    tm = _pick_tile(m, tm_target)
    tn = _pick_tile(n, tn_target, LANES)
    return pl.pallas_call(
        _norm_matmul_kernel,
        grid=(m // tm, n // tn),
        in_specs=[pl.BlockSpec((tm, k), lambda i, j: (i, 0)),
                  pl.BlockSpec((1, k), lambda i, j: (0, 0)),
                  pl.BlockSpec((k, tn), lambda i, j: (0, j))],
        out_specs=pl.BlockSpec((tm, tn), lambda i, j: (i, j)),
        out_shape=jax.ShapeDtypeStruct((m, n), F32),
        scratch_shapes=[pltpu.VMEM((tm, k), BF16)],
        compiler_params=_cparams(("parallel", "arbitrary")),
        name="norm_matmul",
    )(x, g.reshape(1, k), w)


def _matmul_res_kernel(*refs, n_in):
    a_refs, w_refs = refs[:n_in], refs[n_in:2 * n_in]
    r_ref, o_ref = refs[2 * n_in], refs[2 * n_in + 1]
    acc = _dot(a_refs[0][...], w_refs[0][...])
    for a_ref, w_ref in zip(a_refs[1:], w_refs[1:]):
        acc = acc + _dot(a_ref[...], w_ref[...])
    o_ref[...] = r_ref[...] + acc


def matmul_res(a_list, w_list, res, *, tm_target=640, tn_target=1024):
    m, n = res.shape
    tm = _pick_tile(m, tm_target)
    tn = _pick_tile(n, tn_target, LANES)
    n_in = len(a_list)
    in_specs = ([pl.BlockSpec((tm, a.shape[1]), lambda i, j: (i, 0)) for a in a_list]
                + [pl.BlockSpec((w.shape[0], tn), lambda i, j: (0, j)) for w in w_list]
                + [pl.BlockSpec((tm, tn), lambda i, j: (i, j))])
    return pl.pallas_call(
        functools.partial(_matmul_res_kernel, n_in=n_in),
        grid=(m // tm, n // tn),
        in_specs=in_specs,
        out_specs=pl.BlockSpec((tm, tn), lambda i, j: (i, j)),
        out_shape=jax.ShapeDtypeStruct((m, n), F32),
        compiler_params=_cparams(("parallel", "parallel")),
        name="matmul_res",
    )(*a_list, *w_list, res)


def _ffn_kernel(x_ref, g_ref, wg_ref, wu_ref, wd_ref, o_ref, xn_ref, acc_ref):
    f = pl.program_id(1)

    @pl.when(f == 0)
    def _():
        xn_ref[...] = _rms(x_ref[...], g_ref[...]).astype(BF16)
        acc_ref[...] = jnp.zeros_like(acc_ref)

    xn = xn_ref[...]
    a = _dot(xn, wg_ref[...])
    b = _dot(xn, wu_ref[...])
    hid = (a * jax.nn.sigmoid(a)) * b
    acc_ref[...] += _dot(hid.astype(BF16), wd_ref[...])

    @pl.when(f == pl.num_programs(1) - 1)
    def _():
        o_ref[...] = x_ref[...] + acc_ref[...]


def ffn(x, g, wg, wu, wd, *, tm_target=640, tf_target=512):
    m, d = x.shape
    dff = wg.shape[1]
    tm = _pick_tile(m, tm_target)
    tf = _pick_tile(dff, tf_target, LANES)
    return pl.pallas_call(
        _ffn_kernel,
        grid=(m // tm, dff // tf),
        in_specs=[pl.BlockSpec((tm, d), lambda i, f: (i, 0)),
                  pl.BlockSpec((1, d), lambda i, f: (0, 0)),
                  pl.BlockSpec((d, tf), lambda i, f: (0, f)),
                  pl.BlockSpec((d, tf), lambda i, f: (0, f)),
                  pl.BlockSpec((tf, d), lambda i, f: (f, 0))],
        out_specs=pl.BlockSpec((tm, d), lambda i, f: (i, 0)),
        out_shape=jax.ShapeDtypeStruct((m, d), F32),
        scratch_shapes=[pltpu.VMEM((tm, d), BF16), pltpu.VMEM((tm, d), F32)],
        compiler_params=_cparams(("parallel", "arbitrary")),
        name="ffn",
    )(x, g.reshape(1, d), wg, wu, wd)


def _head_rms_kernel(x_ref, g_ref, o_ref, *, n_heads):
    for h in range(n_heads):
        sl = slice(h * HEAD_DIM, (h + 1) * HEAD_DIM)
        o_ref[:, sl] = _rms(x_ref[:, sl], g_ref[...])


def head_rms(x, g, n_heads):
    r, w = x.shape
    tr = _pick_tile(r, 512, 8)
    return pl.pallas_call(
        functools.partial(_head_rms_kernel, n_heads=n_heads),
        grid=(r // tr,),
        in_specs=[pl.BlockSpec((tr, w), lambda i: (i, 0)),
                  pl.BlockSpec((1, HEAD_DIM), lambda i: (0, 0))],
        out_specs=pl.BlockSpec((tr, w), lambda i: (i, 0)),
        out_shape=jax.ShapeDtypeStruct((r, w), F32),
        compiler_params=_cparams(("parallel",)),
        name="head_rms",
    )(x, g.reshape(1, HEAD_DIM))


def _swap_rope_halves(x, lo_half):
    return jnp.where(lo_half, pltpu.roll(x, LANES - MLA_ROPE // 2, axis=1),
                     pltpu.roll(x, MLA_ROPE // 2, axis=1))


def _post_kernel(z_ref, gmq_ref, gmk_ref, gfq_ref, gfk_ref, gqa_ref, gqn_ref, gqr_ref, gkv_ref,
                 gkr_ref, bf_ref, cos_ref, sin_ref, wuq_ref, wuk_ref,
                 mq_ref, mk_ref, mv_ref, fq_ref, fk_ref, fv_ref, lf_ref, qp_ref, ckv_ref, kr_ref,
                 krp_ref):
    tm = z_ref.shape[0]
    for h in range(MOBA_HEADS):
        sl = slice(h * HEAD_DIM, (h + 1) * HEAD_DIM)
        mq_ref[:, sl] = _rms(z_ref[:, C_MQ + h * HEAD_DIM:C_MQ + (h + 1) * HEAD_DIM], gmq_ref[...])
    mk_ref[...] = _rms(z_ref[:, C_MK:C_MK + HEAD_DIM], gmk_ref[...])
    mv_ref[...] = z_ref[:, C_MV:C_MV + HEAD_DIM]
    for h in range(FOX_HEADS):
        sl = slice(h * HEAD_DIM, (h + 1) * HEAD_DIM)
        fq_ref[:, sl] = _rms(z_ref[:, C_FQ + h * HEAD_DIM:C_FQ + (h + 1) * HEAD_DIM],
                             gfq_ref[...]).astype(BF16)
    fk_ref[...] = _rms(z_ref[:, C_FK:C_FK + HEAD_DIM], gfk_ref[...])
    fv_ref[...] = z_ref[:, C_FV:C_FV + HEAD_DIM]

    cos = cos_ref[...]
    sin = sin_ref[...]
    lane = lax.broadcasted_iota(jnp.int32, (tm, LANES), 1)
    lo_half = (lane & (MLA_ROPE - 1)) < MLA_ROPE // 2
    first64 = lane < MLA_ROPE

    cqa = _rms(z_ref[:, C_QA:C_QA + MLA_Q_LORA], gqa_ref[...]).astype(BF16)
    q = _dot(cqa, wuq_ref[...])
    rope0 = MLA_HEADS * MLA_NOPE
    inv_w = 1.0 / (MLA_NOPE + MLA_ROPE)
    for hp in range(MLA_HEADS // 2):
        rp = q[:, rope0 + hp * LANES:rope0 + (hp + 1) * LANES]
        rp2 = rp * rp
        for sub in range(2):
            h = 2 * hp + sub
            nope = q[:, h * MLA_NOPE:(h + 1) * MLA_NOPE]
            mine = first64 if sub == 0 else jnp.logical_not(first64)
            ss = (jnp.sum(nope * nope, axis=-1, keepdims=True)
                  + jnp.sum(jnp.where(mine, rp2, 0.0), axis=-1, keepdims=True))
            inv = lax.rsqrt(ss * inv_w + EPS)
            nope_n = nope * inv * gqn_ref[...]
            qlat = _dot(nope_n.astype(BF16), wuk_ref[h])
            rope_n = rp * inv * gqr_ref[...]
            roped = rope_n * cos + _swap_rope_halves(rope_n, lo_half) * sin
            if sub == 1:
                roped = pltpu.roll(roped, MLA_ROPE, axis=1)
            roped = jnp.where(first64, roped, 0.0)
            qp_ref[:, h * 256:h * 256 + LANES] = qlat.astype(BF16)
            qp_ref[:, h * 256 + LANES:(h + 1) * 256] = roped.astype(BF16)

    ckv_ref[...] = _rms(z_ref[:, C_CKV:C_CKV + MLA_KV_LORA], gkv_ref[...])
    t = z_ref[:, C_KR:C_KR + LANES]
    ss = jnp.sum(jnp.where(first64, t * t, 0.0), axis=-1, keepdims=True)
    krn = t * lax.rsqrt(ss * (1.0 / MLA_ROPE) + EPS) * gkr_ref[...]
    kr_r = krn * cos + _swap_rope_halves(krn, lo_half) * sin
    kr_ref[...] = kr_r[:, :MLA_ROPE]
    krp_ref[...] = jnp.where(first64, kr_r, 0.0).astype(BF16)
    x = t[:, MLA_ROPE:MLA_ROPE + FOX_HEADS] + bf_ref[...]
    lf_ref[...] = jnp.minimum(x, 0.0) - jnp.log1p(jnp.exp(-jnp.abs(x)))


def post_project(z, lp, cos128, sin128, *, tm_target=320):
    m = z.shape[0]
    tm = _pick_tile(m, tm_target)
    row = lambda w: pl.BlockSpec((tm, w), lambda i: (i, 0))
    const = lambda shp: pl.BlockSpec(shp, lambda i: tuple(0 for _ in shp))
    gains = [lp["g_moba_q"], lp["g_moba_k"], lp["g_fox_q"], lp["g_fox_k"], lp["g_mla_qa"],
             lp["g_qn"], lp["g_qr128"], lp["g_mla_kv"], lp["g_kr128"], lp["b_fox_f"]]
    gains = [g.reshape(1, -1) for g in gains]
    out_shapes = [
        jax.ShapeDtypeStruct((m, 512), F32),
        jax.ShapeDtypeStruct((m, HEAD_DIM), F32),
        jax.ShapeDtypeStruct((m, HEAD_DIM), F32),
        jax.ShapeDtypeStruct((m, 512), BF16),
        jax.ShapeDtypeStruct((m, HEAD_DIM), F32),
        jax.ShapeDtypeStruct((m, HEAD_DIM), F32),
        jax.ShapeDtypeStruct((m, FOX_HEADS), F32),
        jax.ShapeDtypeStruct((m, MLA_HEADS * 256), BF16),
        jax.ShapeDtypeStruct((m, MLA_KV_LORA), F32),
        jax.ShapeDtypeStruct((m, MLA_ROPE), F32),
        jax.ShapeDtypeStruct((m, LANES), BF16),
    ]
    return pl.pallas_call(
        _post_kernel,
        grid=(m // tm,),
        in_specs=([row(IN_PAD)] + [const(g.shape) for g in gains]
                  + [row(LANES), row(LANES), const(lp["w_uq"].shape), const(lp["w_uk"].shape)]),
        out_specs=[row(s.shape[1]) for s in out_shapes],
        out_shape=out_shapes,
        compiler_params=_cparams(("parallel",)),
        name="post_project",
    )(z, *gains, cos128, sin128, lp["w_uq"], lp["w_uk"])


def _stack_heads(x, n_heads, width):
    return jnp.concatenate([x[:, h * width:(h + 1) * width] for h in range(n_heads)], axis=0)


def _softmax_step(s, v_blk, m_ref, l_ref, acc_ref):
    m_old = m_ref[...]
    m_new = jnp.maximum(m_old, jnp.max(s, axis=-1, keepdims=True))
    alpha = jnp.exp(m_old - m_new)
    p = jnp.exp(s - m_new)
    l_ref[...] = alpha * l_ref[...] + jnp.sum(p, axis=-1, keepdims=True)
    acc_ref[...] = alpha * acc_ref[...] + _dot(p.astype(BF16), v_blk)
    m_ref[...] = m_new


def _softmax_step_rows(s, v_blk, m_ref, l_ref, acc_ref, sl):
    m_old = m_ref[sl, :]
    m_new = jnp.maximum(m_old, jnp.max(s, axis=-1, keepdims=True))
    alpha = jnp.exp(m_old - m_new)
    p = jnp.exp(s - m_new)
    l_ref[sl, :] = alpha * l_ref[sl, :] + jnp.sum(p, axis=-1, keepdims=True)
    acc_ref[sl, :] = alpha * acc_ref[sl, :] + _dot(p.astype(BF16), v_blk)
    m_ref[sl, :] = m_new


def _init_softmax(m_ref, l_ref, acc_ref):
    m_ref[...] = jnp.full_like(m_ref, NEG_INF)
    l_ref[...] = jnp.zeros_like(l_ref)
    acc_ref[...] = jnp.zeros_like(acc_ref)


def _causal_mask(rows, blk):
    r = lax.broadcasted_iota(jnp.int32, (rows, blk), 0) & (blk - 1)
    c = lax.broadcasted_iota(jnp.int32, (rows, blk), 1)
    return r >= c


def _moba_prompt_kernel(q_ref, k_ref, v_ref, bdiag_ref, bprev_ref, bfar_ref, o_ref,
                        sel_ref, m_ref, l_ref, acc_ref, *, n_blocks):
    blk = MOBA_BLOCK
    rows = MOBA_HEADS * blk
    qb = pl.program_id(1)
    scale = HEAD_DIM ** -0.5
    qs = _stack_heads(q_ref[0], MOBA_HEADS, HEAD_DIM)
    qsb = qs.astype(BF16)

    means = [jnp.mean(k_ref[0, n * blk:(n + 1) * blk, :], axis=0, keepdims=True) for n in range(n_blocks)]
    kmean = jnp.concatenate(means + [jnp.zeros((LANES - n_blocks, HEAD_DIM), F32)], axis=0)
    gate = _dot_nt(qs, kmean, precision=lax.Precision.HIGHEST)
    lane = lax.broadcasted_iota(jnp.int32, (rows, LANES), 1)
    past = lane < qb
    g = jnp.where(past, gate, NEG_INF)
    cnt = jnp.zeros((rows, LANES), F32)
    for m in range(n_blocks):
        gm = g[:, m:m + 1]
        ahead = (gm > g) | ((gm == g) & (lane > m))
        cnt = cnt + jnp.where(ahead, 1.0, 0.0)
    sel = jnp.where(past & (cnt < float(MOBA_TOPK)), 1.0, 0.0)
    for n in range(n_blocks):
        sel_ref[n] = sel[:, n:n + 1]

    _init_softmax(m_ref, l_ref, acc_ref)

    def step(kb, bias, mask):
        start = pl.multiple_of(kb * blk, blk)
        k_blk = k_ref[0, pl.ds(start, blk), :].astype(BF16)
        v_blk = v_ref[0, pl.ds(start, blk), :].astype(BF16)
        s = _dot_nt(qsb, k_blk) * scale + bias
        s = jnp.where(mask, s, NEG_INF)
        _softmax_step(s, v_blk, m_ref, l_ref, acc_ref)

    step(qb, bdiag_ref[...], _causal_mask(rows, blk))

    @pl.when(qb >= 1)
    def _():
        step(qb - 1, bprev_ref[...], sel_ref[qb - 1] > 0.0)

    def far(kb, carry):
        step(kb, bfar_ref[...], sel_ref[kb] > 0.0)
        return carry

    lax.fori_loop(0, jnp.maximum(qb - 1, 0), far, 0)

    o = acc_ref[...] / l_ref[...]
    for h in range(MOBA_HEADS):
        o_ref[0, :, h * HEAD_DIM:(h + 1) * HEAD_DIM] = o[h * blk:(h + 1) * blk].astype(BF16)


def moba_prompt(mq, mk, mv, bdiag, bprev, bfar):
    b, t, _ = mq.shape
    blk = MOBA_BLOCK
    assert t % blk == 0 and t // blk <= LANES
    n_blocks = t // blk
    rows = MOBA_HEADS * blk
    return pl.pallas_call(
        functools.partial(_moba_prompt_kernel, n_blocks=n_blocks),
        grid=(b, n_blocks),
        in_specs=[pl.BlockSpec((1, blk, MOBA_HEADS * HEAD_DIM), lambda i, j: (i, j, 0)),
                  pl.BlockSpec((1, t, HEAD_DIM), lambda i, j: (i, 0, 0)),
                  pl.BlockSpec((1, t, HEAD_DIM), lambda i, j: (i, 0, 0)),
                  pl.BlockSpec((rows, blk), lambda i, j: (0, 0)),
                  pl.BlockSpec((rows, blk), lambda i, j: (0, 0)),
                  pl.BlockSpec((rows, 1), lambda i, j: (0, 0))],
        out_specs=pl.BlockSpec((1, blk, MOBA_HEADS * HEAD_DIM), lambda i, j: (i, j, 0)),
        out_shape=jax.ShapeDtypeStruct((b, t, MOBA_HEADS * HEAD_DIM), BF16),
        scratch_shapes=[pltpu.VMEM((n_blocks, rows, 1), F32),
                        pltpu.VMEM((rows, 1), F32), pltpu.VMEM((rows, 1), F32),
                        pltpu.VMEM((rows, HEAD_DIM), F32)],
        compiler_params=_cparams(("parallel", "arbitrary")),
        name="moba_prompt",
    )(mq, mk, mv, bdiag, bprev, bfar)


def _fox_cum_kernel(lf_ref, cum_ref, cumt_ref, *, blk):
    x = lf_ref[0]
    t = x.shape[0]
    xt = x.T
    lane = lax.broadcasted_iota(jnp.int32, xt.shape, 1)
    s = 1
    while s < t:
        xt = xt + jnp.where(lane >= s, pltpu.roll(xt, s, axis=1), 0.0)
        s *= 2
    cum_ref[0] = xt.T
    for n in range(t // blk):
        cumt_ref[0, n] = xt[:, n * blk:(n + 1) * blk]


def fox_cum(logf, blk):
    b, t, h = logf.shape
    return pl.pallas_call(
        functools.partial(_fox_cum_kernel, blk=blk),
        grid=(b,),
        in_specs=[pl.BlockSpec((1, t, h), lambda i: (i, 0, 0))],
        out_specs=[pl.BlockSpec((1, t, h), lambda i: (i, 0, 0)),
                   pl.BlockSpec((1, t // blk, h, blk), lambda i: (i, 0, 0, 0))],
        out_shape=[jax.ShapeDtypeStruct((b, t, h), F32),
                   jax.ShapeDtypeStruct((b, t // blk, h, blk), F32)],
        compiler_params=_cparams(("parallel",)),
        name="fox_cum",
    )(logf)


def _fox_prompt_kernel(q_ref, k_ref, v_ref, cum_ref, cumt_ref, o_ref, m_ref, l_ref, acc_ref, *, blk):
    qb = pl.program_id(1)
    scale = HEAD_DIM ** -0.5
    cq = cum_ref[0]
    _init_softmax(m_ref, l_ref, acc_ref)
    causal = _causal_mask(blk, blk)

    def step(kb, diag):
        start = pl.multiple_of(kb * blk, blk)
        k_blk = k_ref[0, pl.ds(start, blk), :].astype(BF16)
        v_blk = v_ref[0, pl.ds(start, blk), :].astype(BF16)
        ck = cumt_ref[0, kb]
        for h in range(FOX_HEADS):
            sl = slice(h * blk, (h + 1) * blk)
            q_h = q_ref[0, :, h * HEAD_DIM:(h + 1) * HEAD_DIM]
            s = _dot_nt(q_h, k_blk) * scale + (cq[:, h:h + 1] - ck[h:h + 1, :])
            if diag:
                s = jnp.where(causal, s, NEG_INF)
            _softmax_step_rows(s, v_blk, m_ref, l_ref, acc_ref, sl)

    step(qb, True)

    def past(kb, carry):
        step(kb, False)
        return carry

    lax.fori_loop(0, qb, past, 0)

    o = acc_ref[...] / l_ref[...]
    for h in range(FOX_HEADS):
        o_ref[0, :, h * HEAD_DIM:(h + 1) * HEAD_DIM] = o[h * blk:(h + 1) * blk].astype(BF16)


def fox_prompt(fq, fk, fv, cum, cumt, blk):
    b, t, _ = fq.shape
    rows = FOX_HEADS * blk
    return pl.pallas_call(
        functools.partial(_fox_prompt_kernel, blk=blk),
        grid=(b, t // blk),
        in_specs=[pl.BlockSpec((1, blk, FOX_HEADS * HEAD_DIM), lambda i, j: (i, j, 0)),
                  pl.BlockSpec((1, t, HEAD_DIM), lambda i, j: (i, 0, 0)),
                  pl.BlockSpec((1, t, HEAD_DIM), lambda i, j: (i, 0, 0)),
                  pl.BlockSpec((1, blk, FOX_HEADS), lambda i, j: (i, j, 0)),
                  pl.BlockSpec((1, t // blk, FOX_HEADS, blk), lambda i, j: (i, 0, 0, 0))],
        out_specs=pl.BlockSpec((1, blk, FOX_HEADS * HEAD_DIM), lambda i, j: (i, j, 0)),
        out_shape=jax.ShapeDtypeStruct((b, t, FOX_HEADS * HEAD_DIM), BF16),
        scratch_shapes=[pltpu.VMEM((rows, 1), F32), pltpu.VMEM((rows, 1), F32),
                        pltpu.VMEM((rows, HEAD_DIM), F32)],
        compiler_params=_cparams(("parallel", "arbitrary")),
        name="fox_prompt",
    )(fq, fk, fv, cum, cumt)


def _mla_prompt_kernel(q_ref, c_ref, krp_ref, wuv_ref, o_ref, m_ref, l_ref, acc_ref, *, blk):
    qb = pl.program_id(1)
    scale = (MLA_NOPE + MLA_ROPE) ** -0.5
    _init_softmax(m_ref, l_ref, acc_ref)
    causal = _causal_mask(blk, blk)

    def step(kb, diag):
        start = pl.multiple_of(kb * blk, blk)
        c_blk = c_ref[0, pl.ds(start, blk), :].astype(BF16)
        k_blk = jnp.concatenate([c_blk, krp_ref[0, pl.ds(start, blk), :]], axis=1)
        for h in range(MLA_HEADS):
            sl = slice(h * blk, (h + 1) * blk)
            s = _dot_nt(q_ref[0, :, h * 256:(h + 1) * 256], k_blk) * scale
            if diag:
                s = jnp.where(causal, s, NEG_INF)
            _softmax_step_rows(s, c_blk, m_ref, l_ref, acc_ref, sl)

    step(qb, True)

    def past(kb, carry):
        step(kb, False)
        return carry

    lax.fori_loop(0, qb, past, 0)

    o_lat = (acc_ref[...] / l_ref[...]).astype(BF16)
    for h in range(MLA_HEADS):
        o_ref[0, :, h * HEAD_DIM:(h + 1) * HEAD_DIM] = _dot(
            o_lat[h * blk:(h + 1) * blk], wuv_ref[h]).astype(BF16)


def mla_prompt(qp, ckv, krp, wuv, blk):
    b, t, _ = qp.shape
    rows = MLA_HEADS * blk
    return pl.pallas_call(
        functools.partial(_mla_prompt_kernel, blk=blk),
        grid=(b, t // blk),
        in_specs=[pl.BlockSpec((1, blk, MLA_HEADS * 256), lambda i, j: (i, j, 0)),
                  pl.BlockSpec((1, t, MLA_KV_LORA), lambda i, j: (i, 0, 0)),
                  pl.BlockSpec((1, t, LANES), lambda i, j: (i, 0, 0)),
                  pl.BlockSpec(wuv.shape, lambda i, j: (0, 0, 0))],
        out_specs=pl.BlockSpec((1, blk, MLA_HEADS * HEAD_DIM), lambda i, j: (i, j, 0)),
        out_shape=jax.ShapeDtypeStruct((b, t, MLA_HEADS * HEAD_DIM), BF16),
        scratch_shapes=[pltpu.VMEM((rows, 1), F32), pltpu.VMEM((rows, 1), F32),
                        pltpu.VMEM((rows, MLA_KV_LORA), F32)],
        compiler_params=_cparams(("parallel", "arbitrary")),
        name="mla_prompt",
    )(qp, ckv, krp, wuv)


def _xattn_kernel(q_ref, g_ref, k_ref, v_ref, o_ref):
    scale = HEAD_DIM ** -0.5
    tq = q_ref.shape[1]
    for h in range(X_HEADS):
        sl = slice(h * HEAD_DIM, (h + 1) * HEAD_DIM)
        qh = _rms(q_ref[0, :, sl], g_ref[...]).astype(BF16)
        if tq < 8:
            qh = jnp.broadcast_to(qh[:1], (8, HEAD_DIM))
        s = _dot_nt(qh, k_ref[0, :, sl].astype(BF16)) * scale
        p = jnp.exp(s - jnp.max(s, axis=-1, keepdims=True))
        p = p / jnp.sum(p, axis=-1, keepdims=True)
        o = _dot(p.astype(BF16), v_ref[0, :, sl].astype(BF16))
        o_ref[0, :, sl] = o[:tq].astype(BF16)


def xattn(xq, g_xq, mem_k, mem_v, *, tq_target=512):
    b, tq_all, w = xq.shape
    mlen = mem_k.shape[1]
    assert tq_all == 1 or tq_all % 8 == 0
    tq = 1 if tq_all == 1 else _pick_tile(tq_all, tq_target, 8)
    return pl.pallas_call(
        _xattn_kernel,
        grid=(b, tq_all // tq),
        in_specs=[pl.BlockSpec((1, tq, w), lambda i, j: (i, j, 0)),
                  pl.BlockSpec((1, HEAD_DIM), lambda i, j: (0, 0)),
                  pl.BlockSpec((1, mlen, w), lambda i, j: (i, 0, 0)),
                  pl.BlockSpec((1, mlen, w), lambda i, j: (i, 0, 0))],
        out_specs=pl.BlockSpec((1, tq, w), lambda i, j: (i, j, 0)),
        out_shape=jax.ShapeDtypeStruct((b, tq_all, w), BF16),
        compiler_params=_cparams(("parallel", "arbitrary")),
        name="xattn",
    )(xq, g_xq.reshape(1, HEAD_DIM), mem_k, mem_v)


def _page_copy(cache_ref, layer, page, buf_ref, slot, i, sem_ref):
    return pltpu.make_async_copy(cache_ref.at[layer, page], buf_ref.at[slot, i], sem_ref.at[slot])


def _issue_chunk(pt_ref, seq, chunk, slot, n_pc, layer, streams):
    for i in range(n_pc):
        page = pt_ref[seq, chunk * n_pc + i]
        for cache_ref, buf_ref, sem_ref in streams:
            _page_copy(cache_ref, layer, page, buf_ref, slot, i, sem_ref).start()


def _wait_chunk(slot, n_pc, layer, streams):
    for i in range(n_pc):
        for cache_ref, buf_ref, sem_ref in streams:
            _page_copy(cache_ref, layer, 0, buf_ref, slot, i, sem_ref).wait()


def _stream_step(pt_ref, n_pc, layer, streams):
    s, c = pl.program_id(0), pl.program_id(1)
    n_s, n_c = pl.num_programs(0), pl.num_programs(1)
    t = s * n_c + c
    slot = lax.rem(t, 2)

    @pl.when(t == 0)
    def _():
        _issue_chunk(pt_ref, s, c, slot, n_pc, layer, streams)

    @pl.when(t + 1 < n_s * n_c)
    def _():
        last_c = c == n_c - 1
        _issue_chunk(pt_ref, jnp.where(last_c, s + 1, s), jnp.where(last_c, 0, c + 1),
                     1 - slot, n_pc, layer, streams)

    _wait_chunk(slot, n_pc, layer, streams)
    return slot


def _finish_decode(sn, v_new, m_ref, l_ref, acc_ref):
    m_old = m_ref[...]
    m_new = jnp.maximum(m_old, sn)
    alpha = jnp.exp(m_old - m_new)
    pn = jnp.exp(sn - m_new)
    l = alpha * l_ref[...] + pn
    return (alpha * acc_ref[...] + pn * v_new) / l


def _dec_fox_kernel(pt_ref, q_ref, kn_ref, vn_ref, bias_ref, kc_ref, vc_ref, o_ref,
                    kbuf, vbuf, ksem, vsem, m_ref, l_ref, acc_ref, *, layer, n_pc):
    c = pl.program_id(1)
    slot = _stream_step(pt_ref, n_pc, layer, [(kc_ref, kbuf, ksem), (vc_ref, vbuf, vsem)])
    scale = HEAD_DIM ** -0.5

    @pl.when(c == 0)
    def _():
        _init_softmax(m_ref, l_ref, acc_ref)

    q = q_ref[0]
    k_c = kbuf[slot].reshape(n_pc * PAGE_SIZE, HEAD_DIM).astype(BF16)
    v_c = vbuf[slot].reshape(n_pc * PAGE_SIZE, HEAD_DIM).astype(BF16)
    s = _dot_nt(q, k_c) * scale + bias_ref[0]
    _softmax_step(s, v_c, m_ref, l_ref, acc_ref)

    @pl.when(c == pl.num_programs(1) - 1)
    def _():
        sn = jnp.sum(q.astype(F32) * kn_ref[0], axis=-1, keepdims=True) * scale
        o_ref[0] = _finish_decode(sn, vn_ref[0], m_ref, l_ref, acc_ref)


def dec_fox(pt, q, k_new, v_new, bias, k_cache, v_cache, layer, n_pc):
    s_n, n_pages = pt.shape
    n_c = n_pages // n_pc
    ck = n_pc * PAGE_SIZE
    grid_spec = pltpu.PrefetchScalarGridSpec(
        num_scalar_prefetch=1,
        grid=(s_n, n_c),
        in_specs=[pl.BlockSpec((1, FOX_HEADS, HEAD_DIM), lambda s, c, pt: (s, 0, 0)),
                  pl.BlockSpec((1, 1, HEAD_DIM), lambda s, c, pt: (s, 0, 0)),
                  pl.BlockSpec((1, 1, HEAD_DIM), lambda s, c, pt: (s, 0, 0)),
                  pl.BlockSpec((1, FOX_HEADS, ck), lambda s, c, pt: (s, 0, c)),
                  pl.BlockSpec(memory_space=pl.ANY),
                  pl.BlockSpec(memory_space=pl.ANY)],
        out_specs=pl.BlockSpec((1, FOX_HEADS, HEAD_DIM), lambda s, c, pt: (s, 0, 0)),
        scratch_shapes=[pltpu.VMEM((2, n_pc, PAGE_SIZE, HEAD_DIM), F32),
                        pltpu.VMEM((2, n_pc, PAGE_SIZE, HEAD_DIM), F32),
                        pltpu.SemaphoreType.DMA((2,)), pltpu.SemaphoreType.DMA((2,)),
                        pltpu.VMEM((FOX_HEADS, 1), F32), pltpu.VMEM((FOX_HEADS, 1), F32),
                        pltpu.VMEM((FOX_HEADS, HEAD_DIM), F32)])
    return pl.pallas_call(
        functools.partial(_dec_fox_kernel, layer=layer, n_pc=n_pc),
        grid_spec=grid_spec,
        out_shape=jax.ShapeDtypeStruct((s_n, FOX_HEADS, HEAD_DIM), F32),
        compiler_params=_cparams(("arbitrary", "arbitrary")),
        name="dec_fox",
    )(pt, q, k_new, v_new, bias, k_cache, v_cache)


def _dec_mla_kernel(pt_ref, q_ref, cn_ref, rn_ref, wuv_ref, cc_ref, rc_ref, o_ref,
                    cbuf, rbuf, csem, rsem, m_ref, l_ref, acc_ref, *, layer, n_pc):
    c = pl.program_id(1)
    slot = _stream_step(pt_ref, n_pc, layer, [(cc_ref, cbuf, csem), (rc_ref, rbuf, rsem)])
    scale = (MLA_NOPE + MLA_ROPE) ** -0.5

    @pl.when(c == 0)
    def _():
        _init_softmax(m_ref, l_ref, acc_ref)

    q = q_ref[0]
    ql = q[:, :MLA_KV_LORA]
    qr = q[:, MLA_KV_LORA:MLA_KV_LORA + MLA_ROPE]
    c_c = cbuf[slot].reshape(n_pc * PAGE_SIZE, MLA_KV_LORA).astype(BF16)
    r_c = rbuf[slot].reshape(n_pc * PAGE_SIZE, MLA_ROPE).astype(BF16)
    s = (_dot_nt(ql, c_c) + _dot_nt(qr, r_c)) * scale
    _softmax_step(s, c_c, m_ref, l_ref, acc_ref)

    @pl.when(c == pl.num_programs(1) - 1)
    def _():
        sn = (jnp.sum(ql.astype(F32) * cn_ref[0], axis=-1, keepdims=True)
              + jnp.sum(qr.astype(F32) * rn_ref[0], axis=-1, keepdims=True)) * scale
        o_lat = _finish_decode(sn, cn_ref[0], m_ref, l_ref, acc_ref).astype(BF16)
        row = lax.broadcasted_iota(jnp.int32, (MLA_HEADS, HEAD_DIM), 0)
        for h in range(MLA_HEADS):
            full = _dot(o_lat, wuv_ref[h])
            o_ref[0, :, h * HEAD_DIM:(h + 1) * HEAD_DIM] = jnp.sum(
                jnp.where(row == h, full, 0.0), axis=0, keepdims=True)


def dec_mla(pt, qp, c_new, r_new, wuv, c_cache, r_cache, layer, n_pc):
    s_n, n_pages = pt.shape
    n_c = n_pages // n_pc
    grid_spec = pltpu.PrefetchScalarGridSpec(
        num_scalar_prefetch=1,
        grid=(s_n, n_c),
        in_specs=[pl.BlockSpec((1, MLA_HEADS, 256), lambda s, c, pt: (s, 0, 0)),
                  pl.BlockSpec((1, 1, MLA_KV_LORA), lambda s, c, pt: (s, 0, 0)),
                  pl.BlockSpec((1, 1, MLA_ROPE), lambda s, c, pt: (s, 0, 0)),
                  pl.BlockSpec(wuv.shape, lambda s, c, pt: (0, 0, 0)),
                  pl.BlockSpec(memory_space=pl.ANY),
                  pl.BlockSpec(memory_space=pl.ANY)],
        out_specs=pl.BlockSpec((1, 1, MLA_HEADS * HEAD_DIM), lambda s, c, pt: (s, 0, 0)),
        scratch_shapes=[pltpu.VMEM((2, n_pc, PAGE_SIZE, MLA_KV_LORA), F32),
                        pltpu.VMEM((2, n_pc, PAGE_SIZE, MLA_ROPE), F32),
                        pltpu.SemaphoreType.DMA((2,)), pltpu.SemaphoreType.DMA((2,)),
                        pltpu.VMEM((MLA_HEADS, 1), F32), pltpu.VMEM((MLA_HEADS, 1), F32),
                        pltpu.VMEM((MLA_HEADS, MLA_KV_LORA), F32)])
    return pl.pallas_call(
        functools.partial(_dec_mla_kernel, layer=layer, n_pc=n_pc),
        grid_spec=grid_spec,
        out_shape=jax.ShapeDtypeStruct((s_n, 1, MLA_HEADS * HEAD_DIM), F32),
        compiler_params=_cparams(("arbitrary", "arbitrary")),
        name="dec_mla",
    )(pt, qp, c_new, r_new, wuv, c_cache, r_cache)


def _dec_logf_kernel(pt_ref, ln_ref, lc_ref, o_ref, lbuf, lsem, *, layer, n_pages):
    slot = _stream_step(pt_ref, n_pages, layer, [(lc_ref, lbuf, lsem)])
    n = n_pages * PAGE_SIZE
    xt = lbuf[slot].reshape(n, FOX_HEADS).T
    lane = lax.broadcasted_iota(jnp.int32, xt.shape, 1)
    s = 1
    while s < n:
        xt = xt + jnp.where(lane >= s, pltpu.roll(xt, s, axis=1), 0.0)
        s *= 2
    total = xt[:, n - 1:n] + ln_ref[0]
    o_ref[0] = total - xt


def dec_logf(pt, lf_new, lf_cache, layer):
    s_n, n_pages = pt.shape
    n = n_pages * PAGE_SIZE
    grid_spec = pltpu.PrefetchScalarGridSpec(
        num_scalar_prefetch=1,
        grid=(s_n, 1),
        in_specs=[pl.BlockSpec((1, FOX_HEADS, 1), lambda s, c, pt: (s, 0, 0)),
                  pl.BlockSpec(memory_space=pl.ANY)],
        out_specs=pl.BlockSpec((1, FOX_HEADS, n), lambda s, c, pt: (s, 0, 0)),
        scratch_shapes=[pltpu.VMEM((2, n_pages, PAGE_SIZE, FOX_HEADS), F32),
                        pltpu.SemaphoreType.DMA((2,))])
    return pl.pallas_call(
        functools.partial(_dec_logf_kernel, layer=layer, n_pages=n_pages),
        grid_spec=grid_spec,
        out_shape=jax.ShapeDtypeStruct((s_n, FOX_HEADS, n), F32),
        compiler_params=_cparams(("arbitrary", "arbitrary")),
        name="dec_logf",
    )(pt, lf_new, lf_cache)


def _dec_moba_gate_kernel(pt_ref, q_ref, kc_ref, o_ref, kbuf, ksem, ksum_ref, *, layer, n_pc, n_blocks):
    c = pl.program_id(1)
    slot = _stream_step(pt_ref, n_pc, layer, [(kc_ref, kbuf, ksem)])
    bpc = n_pc * PAGE_SIZE // MOBA_BLOCK

    @pl.when(c == 0)
    def _():
        ksum_ref[...] = jnp.zeros_like(ksum_ref)

    sums = jnp.sum(kbuf[slot].reshape(bpc, MOBA_BLOCK, HEAD_DIM), axis=1)
    ksum_ref[pl.ds(pl.multiple_of(c * bpc, bpc), bpc), :] = sums

    @pl.when(c == pl.num_programs(1) - 1)
    def _():
        kmean = ksum_ref[...] * (1.0 / MOBA_BLOCK)
        gate = _dot_nt(q_ref[0], kmean, precision=lax.Precision.HIGHEST)
        lane = lax.broadcasted_iota(jnp.int32, gate.shape, 1)
        past = lane < n_blocks
        g = jnp.where(past, gate, NEG_INF)
        cnt = jnp.zeros(gate.shape, F32)
        for m in range(n_blocks):
            gm = g[:, m:m + 1]
            cnt = cnt + jnp.where((gm > g) | ((gm == g) & (lane > m)), 1.0, 0.0)
        lane_f = lane.astype(F32)
        out = jnp.zeros(gate.shape, F32)
        for j in range(MOBA_TOPK):
            idx = jnp.sum(jnp.where(past & (cnt == float(j)), lane_f, 0.0), axis=-1, keepdims=True)
            out = jnp.where(lane == j, idx, out)
        o_ref[0] = out.astype(jnp.int32)


def dec_moba_gate(pt, q, k_cache, layer, n_pc):
    s_n, n_pages = pt.shape
    n_blocks = n_pages * PAGE_SIZE // MOBA_BLOCK
    assert n_blocks <= LANES and n_blocks >= MOBA_TOPK and n_pc % 2 == 0
    grid_spec = pltpu.PrefetchScalarGridSpec(
        num_scalar_prefetch=1,
        grid=(s_n, n_pages // n_pc),
        in_specs=[pl.BlockSpec((1, 8, HEAD_DIM), lambda s, c, pt: (s, 0, 0)),
                  pl.BlockSpec(memory_space=pl.ANY)],
        out_specs=pl.BlockSpec((1, 8, LANES), lambda s, c, pt: (s, 0, 0)),
        scratch_shapes=[pltpu.VMEM((2, n_pc, PAGE_SIZE, HEAD_DIM), F32),
                        pltpu.SemaphoreType.DMA((2,)),
                        pltpu.VMEM((LANES, HEAD_DIM), F32)])
    return pl.pallas_call(
        functools.partial(_dec_moba_gate_kernel, layer=layer, n_pc=n_pc, n_blocks=n_blocks),
        grid_spec=grid_spec,
        out_shape=jax.ShapeDtypeStruct((s_n, 8, LANES), jnp.int32),
        compiler_params=_cparams(("arbitrary", "arbitrary")),
        name="dec_moba_gate",
    )(pt, q, k_cache)


def _dec_moba_att_kernel(pt_ref, sel_ref, q_ref, kn_ref, vn_ref, bias_ref, bown_ref, kc_ref, vc_ref,
                         o_ref, kbuf, vbuf, ksem, vsem, *, layer):
    n_sel = MOBA_HEADS * MOBA_TOPK
    ppb = MOBA_BLOCK // PAGE_SIZE
    s = pl.program_id(0)
    n_s = pl.num_programs(0)
    slot = lax.rem(s, 2)
    streams = [(kc_ref, kbuf, ksem), (vc_ref, vbuf, vsem)]

    def issue(seq, slt):
        for p in range(n_sel):
            blk = sel_ref[seq, p]
            for e in range(ppb):
                page = pt_ref[seq, blk * ppb + e]
                for cache_ref, buf_ref, sem_ref in streams:
                    _page_copy(cache_ref, layer, page, buf_ref, slt, p * ppb + e, sem_ref).start()

    @pl.when(s == 0)
    def _():
        issue(s, slot)

    @pl.when(s + 1 < n_s)
    def _():
        issue(s + 1, 1 - slot)

    _wait_chunk(slot, n_sel * ppb, layer, streams)

    scale = HEAD_DIM ** -0.5
    q = q_ref[0]
    row = lax.broadcasted_iota(jnp.int32, (8, MOBA_BLOCK), 0)
    sc = []
    for j in range(MOBA_TOPK):
        sj = jnp.zeros((8, MOBA_BLOCK), F32)
        for h in range(MOBA_HEADS):
            p = h * MOBA_TOPK + j
            k_p = kbuf[slot, p * ppb:(p + 1) * ppb].reshape(MOBA_BLOCK, HEAD_DIM).astype(BF16)
            s_p = _dot_nt(q, k_p) * scale + bias_ref[sel_ref[s, p]]
            sj = jnp.where(row == h, s_p, sj)
        sc.append(sj)
    s_own = jnp.sum(q.astype(F32) * kn_ref[0], axis=-1, keepdims=True) * scale + bown_ref[...]
    m = s_own
    for sj in sc:
        m = jnp.maximum(m, jnp.max(sj, axis=-1, keepdims=True))
    p_own = jnp.exp(s_own - m)
    l = p_own
    acc = p_own * vn_ref[0]
    for j in range(MOBA_TOPK):
        pj = jnp.exp(sc[j] - m)
        l = l + jnp.sum(pj, axis=-1, keepdims=True)
        for h in range(MOBA_HEADS):
            p = h * MOBA_TOPK + j
            v_p = vbuf[slot, p * ppb:(p + 1) * ppb].reshape(MOBA_BLOCK, HEAD_DIM).astype(BF16)
            acc = acc + _dot(jnp.where(row == h, pj, 0.0).astype(BF16), v_p)
    o_ref[0] = acc / l


def dec_moba_att(pt, sel, q, k_new, v_new, bias_blk, bias_own, k_cache, v_cache, layer):
    s_n = pt.shape[0]
    n_blocks = bias_blk.shape[0]
    n_buf = MOBA_HEADS * MOBA_TOPK * (MOBA_BLOCK // PAGE_SIZE)
    grid_spec = pltpu.PrefetchScalarGridSpec(
        num_scalar_prefetch=2,
        grid=(s_n,),
        in_specs=[pl.BlockSpec((1, 8, HEAD_DIM), lambda s, pt, sel: (s, 0, 0)),
                  pl.BlockSpec((1, 1, HEAD_DIM), lambda s, pt, sel: (s, 0, 0)),
                  pl.BlockSpec((1, 1, HEAD_DIM), lambda s, pt, sel: (s, 0, 0)),
                  pl.BlockSpec((n_blocks, 8, MOBA_BLOCK), lambda s, pt, sel: (0, 0, 0)),
                  pl.BlockSpec((8, 1), lambda s, pt, sel: (0, 0)),
                  pl.BlockSpec(memory_space=pl.ANY),
                  pl.BlockSpec(memory_space=pl.ANY)],
        out_specs=pl.BlockSpec((1, 8, HEAD_DIM), lambda s, pt, sel: (s, 0, 0)),
        scratch_shapes=[pltpu.VMEM((2, n_buf, PAGE_SIZE, HEAD_DIM), F32),
                        pltpu.VMEM((2, n_buf, PAGE_SIZE, HEAD_DIM), F32),
                        pltpu.SemaphoreType.DMA((2,)), pltpu.SemaphoreType.DMA((2,))])
    return pl.pallas_call(
        functools.partial(_dec_moba_att_kernel, layer=layer),
        grid_spec=grid_spec,
        out_shape=jax.ShapeDtypeStruct((s_n, 8, HEAD_DIM), F32),
        compiler_params=_cparams(("arbitrary",)),
        name="dec_moba_att",
    )(pt, sel, q, k_new, v_new, bias_blk, bias_own, k_cache, v_cache)


def _t5_bucket(dist):
    n = jnp.maximum(dist, 0)
    max_exact = N_BUCKETS // 2
    large = max_exact + (jnp.log(jnp.maximum(n, 1).astype(F32) / max_exact)
                         / math.log(MAX_DISTANCE / max_exact) * (N_BUCKETS - max_exact)).astype(jnp.int32)
    return jnp.where(n < max_exact, n, jnp.minimum(large, N_BUCKETS - 1))


def _far_bucket_is_constant(min_dist):
    max_exact = N_BUCKETS // 2
    v = max_exact + math.log(min_dist / max_exact) / math.log(MAX_DISTANCE / max_exact) * (N_BUCKETS - max_exact)
    return v >= N_BUCKETS - 1 + 0.5


def _rope_tables(pos):
    half = MLA_ROPE // 2
    inv = ROPE_THETA ** (-jnp.arange(half, dtype=F32) / half)
    ang = pos.astype(F32)[:, None] * inv[None, :]
    cos, sin = jnp.cos(ang), jnp.sin(ang)
    cos128 = jnp.concatenate([cos, cos, cos, cos], axis=1)
    sin128 = jnp.concatenate([-sin, sin, -sin, sin], axis=1)
    return cos128, sin128


def _layer_params(l, p):
    w_in = p["w_in"][l]
    cs = np.cumsum([0, 512, 128, 128, 512, 128, 128, FOX_HEADS, MLA_Q_LORA, MLA_KV_LORA, MLA_ROPE])
    seg = [w_in[:, cs[i]:cs[i + 1]] for i in range(10)]
    mq, mk, mv, fq, fk, fv, fz, cqa, ckv, kr = seg
    pad = jnp.zeros((w_in.shape[0], IN_PAD - C_FZ - FOX_HEADS), w_in.dtype)
    w_in_r = jnp.concatenate([mq, mk, mv, fq, fk, fv, cqa, ckv, kr, fz, pad], axis=1).astype(BF16)
    d_q = MLA_NOPE + MLA_ROPE
    w_uq = p["w_mla_uq"][l].reshape(MLA_Q_LORA, MLA_HEADS, d_q)
    w_uq_r = jnp.concatenate([w_uq[:, :, :MLA_NOPE].reshape(MLA_Q_LORA, -1),
                              w_uq[:, :, MLA_NOPE:].reshape(MLA_Q_LORA, -1)], axis=1).astype(BF16)
    g_q = p["g_mla_q"][l]
    g_kr = p["g_mla_kr"][l]
    return {
        "g_mix": p["g_mix"][l], "w_in": w_in_r, "b_fox_f": p["b_fox_f"][l],
        "g_moba_q": p["g_moba_q"][l], "g_moba_k": p["g_moba_k"][l],
        "g_fox_q": p["g_fox_q"][l], "g_fox_k": p["g_fox_k"][l],
        "g_mla_qa": p["g_mla_qa"][l], "w_uq": w_uq_r,
        "g_qn": g_q[:MLA_NOPE], "g_qr128": jnp.concatenate([g_q[MLA_NOPE:], g_q[MLA_NOPE:]]),
        "g_mla_kv": p["g_mla_kv"][l],
        "g_kr128": jnp.concatenate([g_kr, jnp.zeros((LANES - MLA_ROPE,), g_kr.dtype)]),
        "w_uk": jnp.transpose(p["w_mla_uk"][l], (1, 2, 0)).astype(BF16),
        "w_uv": jnp.transpose(p["w_mla_uv"][l], (1, 0, 2)).astype(BF16),
        "w_o": p["w_o"][l].astype(BF16),
        "g_x": p["g_x"][l], "g_mem": p["g_mem"][l],
        "w_xq": p["w_xq"][l].astype(BF16),
        "w_xkv": jnp.concatenate([p["w_xk"][l], p["w_xv"][l]], axis=1).astype(BF16),
        "g_xq": p["g_xq"][l], "g_xk": p["g_xk"][l],
        "w_xo": p["w_xo"][l].astype(BF16),
        "g_ffn": p["g_ffn"][l],
        "w_gate": p["w_gate"][l].astype(BF16), "w_up": p["w_up"][l].astype(BF16),
        "w_down": p["w_down"][l].astype(BF16),
    }


def kernel(x_prompt, x_sample, cache_moba_k, cache_moba_v, cache_fox_k, cache_fox_v, cache_fox_logf, cache_mla_ckv, cache_mla_krope, cache_mem_k, cache_mem_v, page_table, mem_prompt, rel_bias, g_mix, w_in, b_fox_f, g_moba_q, g_moba_k, g_fox_q, g_fox_k, g_mla_qa, w_mla_uq, g_mla_q, g_mla_kv, g_mla_kr, w_mla_uk, w_mla_uv, w_o, g_x, g_mem, w_xq, w_xk, w_xv, g_xq, g_xk, w_xo, g_ffn, w_gate, w_up, w_down):
    b, t, d = x_prompt.shape
    s_n = x_sample.shape[0]
    assert x_sample.shape[1] == 1
    depth = w_in.shape[0]
    n_pages = page_table.shape[1]
    past_len = n_pages * PAGE_SIZE
    mem_len = mem_prompt.shape[1]
    n_p = b * t
    blk = MOBA_BLOCK
    n_pc = _pick_tile(n_pages, 32, 2)
    params = dict(w_in=w_in, w_mla_uq=w_mla_uq, g_mla_q=g_mla_q, g_mla_kr=g_mla_kr, g_mix=g_mix,
                  b_fox_f=b_fox_f, g_moba_q=g_moba_q, g_moba_k=g_moba_k, g_fox_q=g_fox_q,
                  g_fox_k=g_fox_k, g_mla_qa=g_mla_qa, g_mla_kv=g_mla_kv, w_mla_uk=w_mla_uk,
                  w_mla_uv=w_mla_uv, w_o=w_o, g_x=g_x, g_mem=g_mem, w_xq=w_xq, w_xk=w_xk, w_xv=w_xv,
                  g_xq=g_xq, g_xk=g_xk, w_xo=w_xo, g_ffn=g_ffn, w_gate=w_gate, w_up=w_up,
                  w_down=w_down)

    pos = jnp.concatenate([jnp.tile(jnp.arange(t, dtype=jnp.int32), b),
                           jnp.full((s_n,), past_len, jnp.int32)])
    cos128, sin128 = _rope_tables(pos)
    assert _far_bucket_is_constant(blk + 1)
    bias_tab = rel_bias.T
    ii = jnp.arange(blk)[:, None] - jnp.arange(blk)[None, :]
    rows = MOBA_HEADS * blk
    bdiag = bias_tab[:, _t5_bucket(ii)].reshape(rows, blk)
    bprev = bias_tab[:, _t5_bucket(ii + blk)].reshape(rows, blk)
    bfar = jnp.repeat(bias_tab[:, N_BUCKETS - 1], blk).reshape(rows, 1)
    n_sblk = past_len // blk
    kpos = jnp.arange(past_len).reshape(n_sblk, blk)
    bias_blk = jnp.transpose(bias_tab[:, _t5_bucket(past_len - kpos)], (1, 0, 2))
    bias_blk = jnp.concatenate([bias_blk, jnp.zeros_like(bias_blk)], axis=1)
    bias_own = jnp.concatenate([bias_tab[:, 0], jnp.zeros((8 - MOBA_HEADS,), F32)]).reshape(8, 1)

    h = jnp.concatenate([x_prompt.reshape(n_p, d), x_sample.reshape(s_n, d)], axis=0)
    mem_x = mem_prompt.reshape(b * mem_len, d)
    rows_p, rows_s, mem_ks, mem_vs = [], [], [], []
    pad_heads = lambda a: jnp.concatenate([a, jnp.zeros_like(a)], axis=1)

    for l in range(depth):
        lp = _layer_params(l, params)
        z = norm_matmul(h, lp["g_mix"], lp["w_in"])
        mq, mk, mv, fq, fk, fv, lf, qp, ckv, kr, krp = post_project(z, lp, cos128, sin128)
        new_rows = (mk, mv, fk, fv, lf, ckv, kr)
        rows_p.append(tuple(a[:n_p].reshape(b, t, -1) for a in new_rows))
        rows_s.append(tuple(a[n_p:].reshape(s_n, 1, -1) for a in new_rows))
        pr = lambda a: a[:n_p].reshape(b, t, -1)
        sa = lambda a: a[n_p:]

        o_moba_p = moba_prompt(pr(mq), pr(mk), pr(mv), bdiag, bprev, bfar)
        cum, cumt = fox_cum(pr(lf), blk)
        o_fox_p = fox_prompt(pr(fq), pr(fk), pr(fv), cum, cumt, blk)
        o_mla_p = mla_prompt(pr(qp), pr(ckv), pr(krp), lp["w_uv"], blk)

        mq_s = pad_heads(sa(mq).reshape(s_n, MOBA_HEADS, HEAD_DIM))
        sel = dec_moba_gate(page_table, mq_s, cache_moba_k, l, n_pc)
        sel12 = sel[:, :MOBA_HEADS, :MOBA_TOPK].reshape(s_n, MOBA_HEADS * MOBA_TOPK)
        o_moba_s = dec_moba_att(page_table, sel12, mq_s.astype(BF16), sa(mk).reshape(s_n, 1, -1),
                                sa(mv).reshape(s_n, 1, -1), bias_blk, bias_own,
                                cache_moba_k, cache_moba_v, l)[:, :MOBA_HEADS]
        fbias = dec_logf(page_table, sa(lf).reshape(s_n, FOX_HEADS, 1), cache_fox_logf, l)
        o_fox_s = dec_fox(page_table, sa(fq).reshape(s_n, FOX_HEADS, HEAD_DIM),
                          sa(fk).reshape(s_n, 1, -1), sa(fv).reshape(s_n, 1, -1), fbias,
                          cache_fox_k, cache_fox_v, l, n_pc)
        o_mla_s = dec_mla(page_table, sa(qp).reshape(s_n, MLA_HEADS, 256),
                          sa(ckv).reshape(s_n, 1, -1), sa(kr).reshape(s_n, 1, -1), lp["w_uv"],
                          cache_mla_ckv, cache_mla_krope, l, n_pc)

        o_moba = jnp.concatenate([o_moba_p.reshape(n_p, -1), o_moba_s.reshape(s_n, -1).astype(BF16)])
        o_fox = jnp.concatenate([o_fox_p.reshape(n_p, -1), o_fox_s.reshape(s_n, -1).astype(BF16)])
        o_mla = jnp.concatenate([o_mla_p.reshape(n_p, -1), o_mla_s.reshape(s_n, -1).astype(BF16)])
        w_o_l = lp["w_o"]
        h = matmul_res([o_moba, o_fox, o_mla], [w_o_l[:512], w_o_l[512:1024], w_o_l[1024:]], h)

        kv = norm_matmul(mem_x, lp["g_mem"], lp["w_xkv"], tm_target=512, tn_target=512)
        xw = X_HEADS * HEAD_DIM
        mem_k = head_rms(kv[:, :xw], lp["g_xk"], X_HEADS)
        mem_v = kv[:, xw:]
        mem_ks.append(mem_k.reshape(b, mem_len, X_HEADS, HEAD_DIM))
        mem_vs.append(mem_v.reshape(b, mem_len, X_HEADS, HEAD_DIM))
        xq = norm_matmul(h, lp["g_x"], lp["w_xq"], tn_target=512)
        ox_p = xattn(xq[:n_p].reshape(b, t, xw), lp["g_xq"], mem_k.reshape(b, mem_len, xw),
                     mem_v.reshape(b, mem_len, xw))
        ox_s = xattn(xq[n_p:].reshape(s_n, 1, xw), lp["g_xq"], cache_mem_k[l].reshape(s_n, mem_len, xw),
                     cache_mem_v[l].reshape(s_n, mem_len, xw))
        ox = jnp.concatenate([ox_p.reshape(n_p, xw), ox_s.reshape(s_n, xw)])
        h = matmul_res([ox], [lp["w_xo"]], h)

        h = ffn(h, lp["g_ffn"], lp["w_gate"], lp["w_up"], lp["w_down"])

    stk = lambda rows, i: jnp.stack([r[i] for r in rows])
    return (h[:n_p].reshape(b, t, d), h[n_p:].reshape(s_n, 1, d),
            stk(rows_p, 0), stk(rows_p, 1), stk(rows_p, 2), stk(rows_p, 3),
            stk(rows_p, 4), stk(rows_p, 5), stk(rows_p, 6),
            jnp.stack(mem_ks), jnp.stack(mem_vs),
            stk(rows_s, 0), stk(rows_s, 1), stk(rows_s, 2), stk(rows_s, 3),
            stk(rows_s, 4), stk(rows_s, 5), stk(rows_s, 6))
```

```python
import functools
import math

import numpy as np
import jax
import jax.numpy as jnp
from jax import lax
from jax.experimental import pallas as pl
from jax.experimental.pallas import tpu as pltpu

F32 = jnp.float32
BF16 = jnp.bfloat16
NEG_INF = float("-inf")

EPS = 1e-6
HEAD_DIM = 128
LANES = 128
MOBA_HEADS = 4
FOX_HEADS = 4
MLA_HEADS = 8
X_HEADS = 4
MOBA_BLOCK = 256
MOBA_TOPK = 3
MLA_Q_LORA = 512
MLA_KV_LORA = 128
MLA_NOPE = 128
MLA_ROPE = 64
ROPE_THETA = 10000.0
N_BUCKETS = 32
MAX_DISTANCE = 128
PAGE_SIZE = 128
VMEM_LIMIT = 56 * 1024 * 1024

C_MQ, C_MK, C_MV = 0, 512, 640
C_FQ, C_FK, C_FV = 768, 1280, 1408
C_QA, C_CKV, C_KR = 1536, 2048, 2176
C_FZ = C_KR + MLA_ROPE
IN_PAD = 2304

_NT = (((1,), (1,)), ((), ()))


def _cparams(sem, vmem=VMEM_LIMIT):
    return pltpu.CompilerParams(dimension_semantics=sem, vmem_limit_bytes=vmem)


def _pick_tile(n, target, mult=16):
    best = None
    for t in range(mult, min(n, target) + 1, mult):
        if n % t == 0:
            best = t
    assert best is not None, (n, target, mult)
    return best


def _rms(x, g):
    ms = jnp.mean(x * x, axis=-1, keepdims=True)
    return x * lax.rsqrt(ms + EPS) * g


def _dot(a, b):
    return jnp.dot(a, b, preferred_element_type=F32)


def _dot_nt(a, b, precision=None):
    return lax.dot_general(a, b, _NT, precision=precision, preferred_element_type=F32)


def _norm_matmul_kernel(x_ref, g_ref, w_ref, o_ref, xn_ref):
    @pl.when(pl.program_id(1) == 0)
    def _():
        xn_ref[...] = _rms(x_ref[...], g_ref[...]).astype(BF16)

    o_ref[...] = _dot(xn_ref[...], w_ref[...])


def norm_matmul(x, g, w, *, tm_target=640, tn_target=768):
    m, k = x.shape
    n = w.shape[1]
    tm = _pick_tile(m, tm_target)
    tn = _pick_tile(n, tn_target, LANES)
    return pl.pallas_call(
        _norm_matmul_kernel,
        grid=(m // tm, n // tn),
        in_specs=[pl.BlockSpec((tm, k), lambda i, j: (i, 0)),
                  pl.BlockSpec((1, k), lambda i, j: (0, 0)),
                  pl.BlockSpec((k, tn), lambda i, j: (0, j))],
        out_specs=pl.BlockSpec((tm, tn), lambda i, j: (i, j)),
        out_shape=jax.ShapeDtypeStruct((m, n), F32),
        scratch_shapes=[pltpu.VMEM((tm, k), BF16)],
        compiler_params=_cparams(("parallel", "arbitrary")),
        name="norm_matmul",
    )(x, g.reshape(1, k), w)


def _matmul_res_kernel(*refs, n_in):
    a_refs, w_refs = refs[:n_in], refs[n_in:2 * n_in]
    r_ref, o_ref = refs[2 * n_in], refs[2 * n_in + 1]
    acc = _dot(a_refs[0][...], w_refs[0][...])
    for a_ref, w_ref in zip(a_refs[1:], w_refs[1:]):
        acc = acc + _dot(a_ref[...], w_ref[...])
    o_ref[...] = r_ref[...] + acc


def matmul_res(a_list, w_list, res, *, tm_target=640, tn_target=1024):
    m, n = res.shape
    tm = _pick_tile(m, tm_target)
    tn = _pick_tile(n, tn_target, LANES)
    n_in = len(a_list)
    in_specs = ([pl.BlockSpec((tm, a.shape[1]), lambda i, j: (i, 0)) for a in a_list]
                + [pl.BlockSpec((w.shape[0], tn), lambda i, j: (0, j)) for w in w_list]
                + [pl.BlockSpec((tm, tn), lambda i, j: (i, j))])
    return pl.pallas_call(
        functools.partial(_matmul_res_kernel, n_in=n_in),
        grid=(m // tm, n // tn),
        in_specs=in_specs,
        out_specs=pl.BlockSpec((tm, tn), lambda i, j: (i, j)),
        out_shape=jax.ShapeDtypeStruct((m, n), F32),
        compiler_params=_cparams(("parallel", "parallel")),
        name="matmul_res",
    )(*a_list, *w_list, res)


def _ffn_kernel(x_ref, g_ref, wg_ref, wu_ref, wd_ref, o_ref, xn_ref, acc_ref):
    f = pl.program_id(1)

    @pl.when(f == 0)
    def _():
        xn_ref[...] = _rms(x_ref[...], g_ref[...]).astype(BF16)
        acc_ref[...] = jnp.zeros_like(acc_ref)

    xn = xn_ref[...]
    a = _dot(xn, wg_ref[...])
    b = _dot(xn, wu_ref[...])
    hid = (a * jax.nn.sigmoid(a)) * b
    acc_ref[...] += _dot(hid.astype(BF16), wd_ref[...])

    @pl.when(f == pl.num_programs(1) - 1)
    def _():
        o_ref[...] = x_ref[...] + acc_ref[...]


def ffn(x, g, wg, wu, wd, *, tm_target=640, tf_target=512):
    m, d = x.shape
    dff = wg.shape[1]
    tm = _pick_tile(m, tm_target)
    tf = _pick_tile(dff, tf_target, LANES)
    return pl.pallas_call(
        _ffn_kernel,
        grid=(m // tm, dff // tf),
        in_specs=[pl.BlockSpec((tm, d), lambda i, f: (i, 0)),
                  pl.BlockSpec((1, d), lambda i, f: (0, 0)),
                  pl.BlockSpec((d, tf), lambda i, f: (0, f)),
                  pl.BlockSpec((d, tf), lambda i, f: (0, f)),
                  pl.BlockSpec((tf, d), lambda i, f: (f, 0))],
        out_specs=pl.BlockSpec((tm, d), lambda i, f: (i, 0)),
        out_shape=jax.ShapeDtypeStruct((m, d), F32),
        scratch_shapes=[pltpu.VMEM((tm, d), BF16), pltpu.VMEM((tm, d), F32)],
        compiler_params=_cparams(("parallel", "arbitrary")),
        name="ffn",
    )(x, g.reshape(1, d), wg, wu, wd)


def _head_rms_kernel(x_ref, g_ref, o_ref, *, n_heads):
    for h in range(n_heads):
        sl = slice(h * HEAD_DIM, (h + 1) * HEAD_DIM)
        o_ref[:, sl] = _rms(x_ref[:, sl], g_ref[...])


def head_rms(x, g, n_heads):
    r, w = x.shape
    tr = _pick_tile(r, 512, 8)
    return pl.pallas_call(
        functools.partial(_head_rms_kernel, n_heads=n_heads),
        grid=(r // tr,),
        in_specs=[pl.BlockSpec((tr, w), lambda i: (i, 0)),
                  pl.BlockSpec((1, HEAD_DIM), lambda i: (0, 0))],
        out_specs=pl.BlockSpec((tr, w), lambda i: (i, 0)),
        out_shape=jax.ShapeDtypeStruct((r, w), F32),
        compiler_params=_cparams(("parallel",)),
        name="head_rms",
    )(x, g.reshape(1, HEAD_DIM))


def _swap_rope_halves(x, lo_half):
    return jnp.where(lo_half, pltpu.roll(x, LANES - MLA_ROPE // 2, axis=1),
                     pltpu.roll(x, MLA_ROPE // 2, axis=1))


def _post_kernel(z_ref, gmq_ref, gmk_ref, gfq_ref, gfk_ref, gqa_ref, gqn_ref, gqr_ref, gkv_ref,
                 gkr_ref, bf_ref, cos_ref, sin_ref, wuq_ref, wuk_ref,
                 mq_ref, mk_ref, mv_ref, fq_ref, fk_ref, fv_ref, lf_ref, qp_ref, ckv_ref, kr_ref,
                 krp_ref):
    tm = z_ref.shape[0]
    for h in range(MOBA_HEADS):
        sl = slice(h * HEAD_DIM, (h + 1) * HEAD_DIM)
        mq_ref[:, sl] = _rms(z_ref[:, C_MQ + h * HEAD_DIM:C_MQ + (h + 1) * HEAD_DIM], gmq_ref[...])
    mk_ref[...] = _rms(z_ref[:, C_MK:C_MK + HEAD_DIM], gmk_ref[...])
    mv_ref[...] = z_ref[:, C_MV:C_MV + HEAD_DIM]
    for h in range(FOX_HEADS):
        sl = slice(h * HEAD_DIM, (h + 1) * HEAD_DIM)
        fq_ref[:, sl] = _rms(z_ref[:, C_FQ + h * HEAD_DIM:C_FQ + (h + 1) * HEAD_DIM],
                             gfq_ref[...]).astype(BF16)
    fk_ref[...] = _rms(z_ref[:, C_FK:C_FK + HEAD_DIM], gfk_ref[...])
    fv_ref[...] = z_ref[:, C_FV:C_FV + HEAD_DIM]

    cos = cos_ref[...]
    sin = sin_ref[...]
    lane = lax.broadcasted_iota(jnp.int32, (tm, LANES), 1)
    lo_half = (lane & (MLA_ROPE - 1)) < MLA_ROPE // 2
    first64 = lane < MLA_ROPE

    cqa = _rms(z_ref[:, C_QA:C_QA + MLA_Q_LORA], gqa_ref[...]).astype(BF16)
    q = _dot(cqa, wuq_ref[...])
    rope0 = MLA_HEADS * MLA_NOPE
    inv_w = 1.0 / (MLA_NOPE + MLA_ROPE)
    for hp in range(MLA_HEADS // 2):
        rp = q[:, rope0 + hp * LANES:rope0 + (hp + 1) * LANES]
        rp2 = rp * rp
        for sub in range(2):
            h = 2 * hp + sub
            nope = q[:, h * MLA_NOPE:(h + 1) * MLA_NOPE]
            mine = first64 if sub == 0 else jnp.logical_not(first64)
            ss = (jnp.sum(nope * nope, axis=-1, keepdims=True)
                  + jnp.sum(jnp.where(mine, rp2, 0.0), axis=-1, keepdims=True))
            inv = lax.rsqrt(ss * inv_w + EPS)
            nope_n = nope * inv * gqn_ref[...]
            qlat = _dot(nope_n.astype(BF16), wuk_ref[h])
            rope_n = rp * inv * gqr_ref[...]
            roped = rope_n * cos + _swap_rope_halves(rope_n, lo_half) * sin
            if sub == 1:
                roped = pltpu.roll(roped, MLA_ROPE, axis=1)
            roped = jnp.where(first64, roped, 0.0)
            qp_ref[:, h * 256:h * 256 + LANES] = qlat.astype(BF16)
            qp_ref[:, h * 256 + LANES:(h + 1) * 256] = roped.astype(BF16)

    ckv_ref[...] = _rms(z_ref[:, C_CKV:C_CKV + MLA_KV_LORA], gkv_ref[...])
    t = z_ref[:, C_KR:C_KR + LANES]
    ss = jnp.sum(jnp.where(first64, t * t, 0.0), axis=-1, keepdims=True)
    krn = t * lax.rsqrt(ss * (1.0 / MLA_ROPE) + EPS) * gkr_ref[...]
    kr_r = krn * cos + _swap_rope_halves(krn, lo_half) * sin
    kr_ref[...] = kr_r[:, :MLA_ROPE]
    krp_ref[...] = jnp.where(first64, kr_r, 0.0).astype(BF16)
    x = t[:, MLA_ROPE:MLA_ROPE + FOX_HEADS] + bf_ref[...]
    lf_ref[...] = jnp.minimum(x, 0.0) - jnp.log1p(jnp.exp(-jnp.abs(x)))


def post_project(z, lp, cos128, sin128, *, tm_target=320):
    m = z.shape[0]
    tm = _pick_tile(m, tm_target)
    row = lambda w: pl.BlockSpec((tm, w), lambda i: (i, 0))
    const = lambda shp: pl.BlockSpec(shp, lambda i: tuple(0 for _ in shp))
    gains = [lp["g_moba_q"], lp["g_moba_k"], lp["g_fox_q"], lp["g_fox_k"], lp["g_mla_qa"],
             lp["g_qn"], lp["g_qr128"], lp["g_mla_kv"], lp["g_kr128"], lp["b_fox_f"]]
    gains = [g.reshape(1, -1) for g in gains]
    out_shapes = [
        jax.ShapeDtypeStruct((m, 512), F32),
        jax.ShapeDtypeStruct((m, HEAD_DIM), F32),
        jax.ShapeDtypeStruct((m, HEAD_DIM), F32),
        jax.ShapeDtypeStruct((m, 512), BF16),
        jax.ShapeDtypeStruct((m, HEAD_DIM), F32),
        jax.ShapeDtypeStruct((m, HEAD_DIM), F32),
        jax.ShapeDtypeStruct((m, FOX_HEADS), F32),
        jax.ShapeDtypeStruct((m, MLA_HEADS * 256), BF16),
        jax.ShapeDtypeStruct((m, MLA_KV_LORA), F32),
        jax.ShapeDtypeStruct((m, MLA_ROPE), F32),
        jax.ShapeDtypeStruct((m, LANES), BF16),
    ]
    return pl.pallas_call(
        _post_kernel,
        grid=(m // tm,),
        in_specs=([row(IN_PAD)] + [const(g.shape) for g in gains]
                  + [row(LANES), row(LANES), const(lp["w_uq"].shape), const(lp["w_uk"].shape)]),
        out_specs=[row(s.shape[1]) for s in out_shapes],
        out_shape=out_shapes,
        compiler_params=_cparams(("parallel",)),
        name="post_project",
    )(z, *gains, cos128, sin128, lp["w_uq"], lp["w_uk"])


def _stack_heads(x, n_heads, width):
    return jnp.concatenate([x[:, h * width:(h + 1) * width] for h in range(n_heads)], axis=0)


def _softmax_step(s, v_blk, m_ref, l_ref, acc_ref):
    m_old = m_ref[...]
    m_new = jnp.maximum(m_old, jnp.max(s, axis=-1, keepdims=True))
    alpha = jnp.exp(m_old - m_new)
    p = jnp.exp(s - m_new)
    l_ref[...] = alpha * l_ref[...] + jnp.sum(p, axis=-1, keepdims=True)
    acc_ref[...] = alpha * acc_ref[...] + _dot(p.astype(BF16), v_blk)
    m_ref[...] = m_new


def _softmax_step_rows(s, v_blk, m_ref, l_ref, acc_ref, sl):
    m_old = m_ref[sl, :]
    m_new = jnp.maximum(m_old, jnp.max(s, axis=-1, keepdims=True))
    alpha = jnp.exp(m_old - m_new)
    p = jnp.exp(s - m_new)
    l_ref[sl, :] = alpha * l_ref[sl, :] + jnp.sum(p, axis=-1, keepdims=True)
    acc_ref[sl, :] = alpha * acc_ref[sl, :] + _dot(p.astype(BF16), v_blk)
    m_ref[sl, :] = m_new


def _init_softmax(m_ref, l_ref, acc_ref):
    m_ref[...] = jnp.full_like(m_ref, NEG_INF)
    l_ref[...] = jnp.zeros_like(l_ref)
    acc_ref[...] = jnp.zeros_like(acc_ref)


def _causal_mask(rows, blk):
    r = lax.broadcasted_iota(jnp.int32, (rows, blk), 0) & (blk - 1)
    c = lax.broadcasted_iota(jnp.int32, (rows, blk), 1)
    return r >= c


def _moba_prompt_kernel(q_ref, k_ref, v_ref, bdiag_ref, bprev_ref, bfar_ref, o_ref,
                        sel_ref, m_ref, l_ref, acc_ref, *, n_blocks):
    blk = MOBA_BLOCK
    rows = MOBA_HEADS * blk
    qb = pl.program_id(1)
    scale = HEAD_DIM ** -0.5
    qs = _stack_heads(q_ref[0], MOBA_HEADS, HEAD_DIM)
    qsb = qs.astype(BF16)

    means = [jnp.mean(k_ref[0, n * blk:(n + 1) * blk, :], axis=0, keepdims=True) for n in range(n_blocks)]
    kmean = jnp.concatenate(means + [jnp.zeros((LANES - n_blocks, HEAD_DIM), F32)], axis=0)
    gate = _dot_nt(qs, kmean, precision=lax.Precision.HIGHEST)
    lane = lax.broadcasted_iota(jnp.int32, (rows, LANES), 1)
    past = lane < qb
    g = jnp.where(past, gate, NEG_INF)
    cnt = jnp.zeros((rows, LANES), F32)
    for m in range(n_blocks):
        gm = g[:, m:m + 1]
        ahead = (gm > g) | ((gm == g) & (lane > m))
        cnt = cnt + jnp.where(ahead, 1.0, 0.0)
    sel = jnp.where(past & (cnt < float(MOBA_TOPK)), 1.0, 0.0)
    for n in range(n_blocks):
        sel_ref[n] = sel[:, n:n + 1]

    _init_softmax(m_ref, l_ref, acc_ref)

    def step(kb, bias, mask):
        start = pl.multiple_of(kb * blk, blk)
        k_blk = k_ref[0, pl.ds(start, blk), :].astype(BF16)
        v_blk = v_ref[0, pl.ds(start, blk), :].astype(BF16)
        s = _dot_nt(qsb, k_blk) * scale + bias
        s = jnp.where(mask, s, NEG_INF)
        _softmax_step(s, v_blk, m_ref, l_ref, acc_ref)

    step(qb, bdiag_ref[...], _causal_mask(rows, blk))

    @pl.when(qb >= 1)
    def _():
        step(qb - 1, bprev_ref[...], sel_ref[qb - 1] > 0.0)

    def far(kb, carry):
        step(kb, bfar_ref[...], sel_ref[kb] > 0.0)
        return carry

    lax.fori_loop(0, jnp.maximum(qb - 1, 0), far, 0)

    o = acc_ref[...] / l_ref[...]
    for h in range(MOBA_HEADS):
        o_ref[0, :, h * HEAD_DIM:(h + 1) * HEAD_DIM] = o[h * blk:(h + 1) * blk].astype(BF16)


def moba_prompt(mq, mk, mv, bdiag, bprev, bfar):
    b, t, _ = mq.shape
    blk = MOBA_BLOCK
    assert t % blk == 0 and t // blk <= LANES
    n_blocks = t // blk
    rows = MOBA_HEADS * blk
    return pl.pallas_call(
        functools.partial(_moba_prompt_kernel, n_blocks=n_blocks),
        grid=(b, n_blocks),
        in_specs=[pl.BlockSpec((1, blk, MOBA_HEADS * HEAD_DIM), lambda i, j: (i, j, 0)),
                  pl.BlockSpec((1, t, HEAD_DIM), lambda i, j: (i, 0, 0)),
                  pl.BlockSpec((1, t, HEAD_DIM), lambda i, j: (i, 0, 0)),
                  pl.BlockSpec((rows, blk), lambda i, j: (0, 0)),
                  pl.BlockSpec((rows, blk), lambda i, j: (0, 0)),
                  pl.BlockSpec((rows, 1), lambda i, j: (0, 0))],
        out_specs=pl.BlockSpec((1, blk, MOBA_HEADS * HEAD_DIM), lambda i, j: (i, j, 0)),
        out_shape=jax.ShapeDtypeStruct((b, t, MOBA_HEADS * HEAD_DIM), BF16),
        scratch_shapes=[pltpu.VMEM((n_blocks, rows, 1), F32),
                        pltpu.VMEM((rows, 1), F32), pltpu.VMEM((rows, 1), F32),
                        pltpu.VMEM((rows, HEAD_DIM), F32)],
        compiler_params=_cparams(("parallel", "arbitrary")),
        name="moba_prompt",
    )(mq, mk, mv, bdiag, bprev, bfar)


def _fox_cum_kernel(lf_ref, cum_ref, cumt_ref, *, blk):
    x = lf_ref[0]
    t = x.shape[0]
    xt = x.T
    lane = lax.broadcasted_iota(jnp.int32, xt.shape, 1)
    s = 1
    while s < t:
        xt = xt + jnp.where(lane >= s, pltpu.roll(xt, s, axis=1), 0.0)
        s *= 2
    cum_ref[0] = xt.T
    for n in range(t // blk):
        cumt_ref[0, n] = xt[:, n * blk:(n + 1) * blk]


def fox_cum(logf, blk):
    b, t, h = logf.shape
    return pl.pallas_call(
        functools.partial(_fox_cum_kernel, blk=blk),
        grid=(b,),
        in_specs=[pl.BlockSpec((1, t, h), lambda i: (i, 0, 0))],
        out_specs=[pl.BlockSpec((1, t, h), lambda i: (i, 0, 0)),
                   pl.BlockSpec((1, t // blk, h, blk), lambda i: (i, 0, 0, 0))],
        out_shape=[jax.ShapeDtypeStruct((b, t, h), F32),
                   jax.ShapeDtypeStruct((b, t // blk, h, blk), F32)],
        compiler_params=_cparams(("parallel",)),
        name="fox_cum",
    )(logf)


def _fox_prompt_kernel(q_ref, k_ref, v_ref, cum_ref, cumt_ref, o_ref, m_ref, l_ref, acc_ref, *, blk):
    qb = pl.program_id(1)
    scale = HEAD_DIM ** -0.5
    cq = cum_ref[0]
    _init_softmax(m_ref, l_ref, acc_ref)
    causal = _causal_mask(blk, blk)

    def step(kb, diag):
        start = pl.multiple_of(kb * blk, blk)
        k_blk = k_ref[0, pl.ds(start, blk), :].astype(BF16)
        v_blk = v_ref[0, pl.ds(start, blk), :].astype(BF16)
        ck = cumt_ref[0, kb]
        for h in range(FOX_HEADS):
            sl = slice(h * blk, (h + 1) * blk)
            q_h = q_ref[0, :, h * HEAD_DIM:(h + 1) * HEAD_DIM]
            s = _dot_nt(q_h, k_blk) * scale + (cq[:, h:h + 1] - ck[h:h + 1, :])
            if diag:
                s = jnp.where(causal, s, NEG_INF)
            _softmax_step_rows(s, v_blk, m_ref, l_ref, acc_ref, sl)

    step(qb, True)

    def past(kb, carry):
        step(kb, False)
        return carry

    lax.fori_loop(0, qb, past, 0)

    o = acc_ref[...] / l_ref[...]
    for h in range(FOX_HEADS):
        o_ref[0, :, h * HEAD_DIM:(h + 1) * HEAD_DIM] = o[h * blk:(h + 1) * blk].astype(BF16)


def fox_prompt(fq, fk, fv, cum, cumt, blk):
    b, t, _ = fq.shape
    rows = FOX_HEADS * blk
    return pl.pallas_call(
        functools.partial(_fox_prompt_kernel, blk=blk),
        grid=(b, t // blk),
        in_specs=[pl.BlockSpec((1, blk, FOX_HEADS * HEAD_DIM), lambda i, j: (i, j, 0)),
                  pl.BlockSpec((1, t, HEAD_DIM), lambda i, j: (i, 0, 0)),
                  pl.BlockSpec((1, t, HEAD_DIM), lambda i, j: (i, 0, 0)),
                  pl.BlockSpec((1, blk, FOX_HEADS), lambda i, j: (i, j, 0)),
                  pl.BlockSpec((1, t // blk, FOX_HEADS, blk), lambda i, j: (i, 0, 0, 0))],
        out_specs=pl.BlockSpec((1, blk, FOX_HEADS * HEAD_DIM), lambda i, j: (i, j, 0)),
        out_shape=jax.ShapeDtypeStruct((b, t, FOX_HEADS * HEAD_DIM), BF16),
        scratch_shapes=[pltpu.VMEM((rows, 1), F32), pltpu.VMEM((rows, 1), F32),
                        pltpu.VMEM((rows, HEAD_DIM), F32)],
        compiler_params=_cparams(("parallel", "arbitrary")),
        name="fox_prompt",
    )(fq, fk, fv, cum, cumt)


def _mla_prompt_kernel(q_ref, c_ref, krp_ref, wuv_ref, o_ref, m_ref, l_ref, acc_ref, *, blk):
    qb = pl.program_id(1)
    scale = (MLA_NOPE + MLA_ROPE) ** -0.5
    _init_softmax(m_ref, l_ref, acc_ref)
    causal = _causal_mask(blk, blk)

    def step(kb, diag):
        start = pl.multiple_of(kb * blk, blk)
        c_blk = c_ref[0, pl.ds(start, blk), :].astype(BF16)
        k_blk = jnp.concatenate([c_blk, krp_ref[0, pl.ds(start, blk), :]], axis=1)
        for h in range(MLA_HEADS):
            sl = slice(h * blk, (h + 1) * blk)
            s = _dot_nt(q_ref[0, :, h * 256:(h + 1) * 256], k_blk) * scale
            if diag:
                s = jnp.where(causal, s, NEG_INF)
            _softmax_step_rows(s, c_blk, m_ref, l_ref, acc_ref, sl)

    step(qb, True)

    def past(kb, carry):
        step(kb, False)
        return carry

    lax.fori_loop(0, qb, past, 0)

    o_lat = (acc_ref[...] / l_ref[...]).astype(BF16)
    for h in range(MLA_HEADS):
        o_ref[0, :, h * HEAD_DIM:(h + 1) * HEAD_DIM] = _dot(
            o_lat[h * blk:(h + 1) * blk], wuv_ref[h]).astype(BF16)


def mla_prompt(qp, ckv, krp, wuv, blk):
    b, t, _ = qp.shape
    rows = MLA_HEADS * blk
    return pl.pallas_call(
        functools.partial(_mla_prompt_kernel, blk=blk),
        grid=(b, t // blk),
        in_specs=[pl.BlockSpec((1, blk, MLA_HEADS * 256), lambda i, j: (i, j, 0)),
                  pl.BlockSpec((1, t, MLA_KV_LORA), lambda i, j: (i, 0, 0)),
                  pl.BlockSpec((1, t, LANES), lambda i, j: (i, 0, 0)),
                  pl.BlockSpec(wuv.shape, lambda i, j: (0, 0, 0))],
        out_specs=pl.BlockSpec((1, blk, MLA_HEADS * HEAD_DIM), lambda i, j: (i, j, 0)),
        out_shape=jax.ShapeDtypeStruct((b, t, MLA_HEADS * HEAD_DIM), BF16),
        scratch_shapes=[pltpu.VMEM((rows, 1), F32), pltpu.VMEM((rows, 1), F32),
                        pltpu.VMEM((rows, MLA_KV_LORA), F32)],
        compiler_params=_cparams(("parallel", "arbitrary")),
        name="mla_prompt",
    )(qp, ckv, krp, wuv)


def _xattn_kernel(q_ref, g_ref, k_ref, v_ref, o_ref):
    scale = HEAD_DIM ** -0.5
    tq = q_ref.shape[1]
    for h in range(X_HEADS):
        sl = slice(h * HEAD_DIM, (h + 1) * HEAD_DIM)
        qh = _rms(q_ref[0, :, sl], g_ref[...]).astype(BF16)
        if tq < 8:
            qh = jnp.broadcast_to(qh[:1], (8, HEAD_DIM))
        s = _dot_nt(qh, k_ref[0, :, sl].astype(BF16)) * scale
        p = jnp.exp(s - jnp.max(s, axis=-1, keepdims=True))
        p = p / jnp.sum(p, axis=-1, keepdims=True)
        o = _dot(p.astype(BF16), v_ref[0, :, sl].astype(BF16))
        o_ref[0, :, sl] = o[:tq].astype(BF16)


def xattn(xq, g_xq, mem_k, mem_v, *, tq_target=512):
    b, tq_all, w = xq.shape
    mlen = mem_k.shape[1]
    assert tq_all == 1 or tq_all % 8 == 0
    tq = 1 if tq_all == 1 else _pick_tile(tq_all, tq_target, 8)
    return pl.pallas_call(
        _xattn_kernel,
        grid=(b, tq_all // tq),
        in_specs=[pl.BlockSpec((1, tq, w), lambda i, j: (i, j, 0)),
                  pl.BlockSpec((1, HEAD_DIM), lambda i, j: (0, 0)),
                  pl.BlockSpec((1, mlen, w), lambda i, j: (i, 0, 0)),
                  pl.BlockSpec((1, mlen, w), lambda i, j: (i, 0, 0))],
        out_specs=pl.BlockSpec((1, tq, w), lambda i, j: (i, j, 0)),
        out_shape=jax.ShapeDtypeStruct((b, tq_all, w), BF16),
        compiler_params=_cparams(("parallel", "arbitrary")),
        name="xattn",
    )(xq, g_xq.reshape(1, HEAD_DIM), mem_k, mem_v)


def _xattn_dec_kernel(q_ref, g_ref, k_ref, v_ref, o_ref):
    scale = HEAD_DIM ** -0.5
    for h in range(X_HEADS):
        sl = slice(h * HEAD_DIM, (h + 1) * HEAD_DIM)
        qh = _rms(q_ref[0, :, sl], g_ref[...]).astype(BF16)
        qh = jnp.broadcast_to(qh, (8, HEAD_DIM))
        s = _dot_nt(qh, k_ref[0, 0, :, h, :].astype(BF16)) * scale
        p = jnp.exp(s - jnp.max(s, axis=-1, keepdims=True))
        p = p / jnp.sum(p, axis=-1, keepdims=True)
        o = _dot(p.astype(BF16), v_ref[0, 0, :, h, :].astype(BF16))
        o_ref[0, :, sl] = o[:1].astype(BF16)


def xattn_dec(xq, g_xq, cache_k, cache_v, layer):
    s_n, _, w = xq.shape
    _, _, mlen, n_h, hd = cache_k.shape
    kv_spec = pl.BlockSpec((1, 1, mlen, n_h, hd), lambda i: (layer, i, 0, 0, 0))
    return pl.pallas_call(
        _xattn_dec_kernel,
        grid=(s_n,),
        in_specs=[pl.BlockSpec((1, 1, w), lambda i: (i, 0, 0)),
                  pl.BlockSpec((1, HEAD_DIM), lambda i: (0, 0)),
                  kv_spec, kv_spec],
        out_specs=pl.BlockSpec((1, 1, w), lambda i: (i, 0, 0)),
        out_shape=jax.ShapeDtypeStruct((s_n, 1, w), BF16),
        compiler_params=_cparams(("parallel",)),
        name="xattn_dec",
    )(xq, g_xq.reshape(1, HEAD_DIM), cache_k, cache_v)


def _page_copy(cache_ref, layer, page, buf_ref, slot, i, sem_ref):
    return pltpu.make_async_copy(cache_ref.at[layer, page], buf_ref.at[slot, i], sem_ref.at[slot])


def _issue_chunk(pt_ref, seq, chunk, slot, n_pc, layer, streams):
    for i in range(n_pc):
        page = pt_ref[seq, chunk * n_pc + i]
        for cache_ref, buf_ref, sem_ref in streams:
            _page_copy(cache_ref, layer, page, buf_ref, slot, i, sem_ref).start()


def _wait_chunk(slot, n_pc, layer, streams):
    for i in range(n_pc):
        for cache_ref, buf_ref, sem_ref in streams:
            _page_copy(cache_ref, layer, 0, buf_ref, slot, i, sem_ref).wait()


def _stream_step(pt_ref, n_pc, layer, streams):
    s, c = pl.program_id(0), pl.program_id(1)
    n_s, n_c = pl.num_programs(0), pl.num_programs(1)
    t = s * n_c + c
    slot = lax.rem(t, 2)

    @pl.when(t == 0)
    def _():
        _issue_chunk(pt_ref, s, c, slot, n_pc, layer, streams)

    @pl.when(t + 1 < n_s * n_c)
    def _():
        last_c = c == n_c - 1
        _issue_chunk(pt_ref, jnp.where(last_c, s + 1, s), jnp.where(last_c, 0, c + 1),
                     1 - slot, n_pc, layer, streams)

    _wait_chunk(slot, n_pc, layer, streams)
    return slot


def _finish_decode(sn, v_new, m_ref, l_ref, acc_ref):
    m_old = m_ref[...]
    m_new = jnp.maximum(m_old, sn)
    alpha = jnp.exp(m_old - m_new)
    pn = jnp.exp(sn - m_new)
    l = alpha * l_ref[...] + pn
    return (alpha * acc_ref[...] + pn * v_new) / l


def _dec_fox_kernel(pt_ref, q_ref, kn_ref, vn_ref, bias_ref, kc_ref, vc_ref, o_ref,
                    kbuf, vbuf, ksem, vsem, m_ref, l_ref, acc_ref, *, layer, n_pc):
    c = pl.program_id(1)
    slot = _stream_step(pt_ref, n_pc, layer, [(kc_ref, kbuf, ksem), (vc_ref, vbuf, vsem)])
    scale = HEAD_DIM ** -0.5

    @pl.when(c == 0)
    def _():
        _init_softmax(m_ref, l_ref, acc_ref)

    q = q_ref[0]
    k_c = kbuf[slot].reshape(n_pc * PAGE_SIZE, HEAD_DIM).astype(BF16)
    v_c = vbuf[slot].reshape(n_pc * PAGE_SIZE, HEAD_DIM).astype(BF16)
    s = _dot_nt(q, k_c) * scale + bias_ref[0]
    _softmax_step(s, v_c, m_ref, l_ref, acc_ref)

    @pl.when(c == pl.num_programs(1) - 1)
    def _():
        sn = jnp.sum(q.astype(F32) * kn_ref[0], axis=-1, keepdims=True) * scale
        o_ref[0] = _finish_decode(sn, vn_ref[0], m_ref, l_ref, acc_ref)


def dec_fox(pt, q, k_new, v_new, bias, k_cache, v_cache, layer, n_pc):
    s_n, n_pages = pt.shape
    n_c = n_pages // n_pc
    ck = n_pc * PAGE_SIZE
    grid_spec = pltpu.PrefetchScalarGridSpec(
        num_scalar_prefetch=1,
        grid=(s_n, n_c),
        in_specs=[pl.BlockSpec((1, FOX_HEADS, HEAD_DIM), lambda s, c, pt: (s, 0, 0)),
                  pl.BlockSpec((1, 1, HEAD_DIM), lambda s, c, pt: (s, 0, 0)),
                  pl.BlockSpec((1, 1, HEAD_DIM), lambda s, c, pt: (s, 0, 0)),
                  pl.BlockSpec((1, FOX_HEADS, ck), lambda s, c, pt: (s, 0, c)),
                  pl.BlockSpec(memory_space=pl.ANY),
                  pl.BlockSpec(memory_space=pl.ANY)],
        out_specs=pl.BlockSpec((1, FOX_HEADS, HEAD_DIM), lambda s, c, pt: (s, 0, 0)),
        scratch_shapes=[pltpu.VMEM((2, n_pc, PAGE_SIZE, HEAD_DIM), F32),
                        pltpu.VMEM((2, n_pc, PAGE_SIZE, HEAD_DIM), F32),
                        pltpu.SemaphoreType.DMA((2,)), pltpu.SemaphoreType.DMA((2,)),
                        pltpu.VMEM((FOX_HEADS, 1), F32), pltpu.VMEM((FOX_HEADS, 1), F32),
                        pltpu.VMEM((FOX_HEADS, HEAD_DIM), F32)])
    return pl.pallas_call(
        functools.partial(_dec_fox_kernel, layer=layer, n_pc=n_pc),
        grid_spec=grid_spec,
        out_shape=jax.ShapeDtypeStruct((s_n, FOX_HEADS, HEAD_DIM), F32),
        compiler_params=_cparams(("arbitrary", "arbitrary")),
        name="dec_fox",
    )(pt, q, k_new, v_new, bias, k_cache, v_cache)


def _dec_mla_kernel(pt_ref, q_ref, cn_ref, rn_ref, wuv_ref, cc_ref, rc_ref, o_ref,
                    cbuf, rbuf, csem, rsem, m_ref, l_ref, acc_ref, *, layer, n_pc):
    c = pl.program_id(1)
    slot = _stream_step(pt_ref, n_pc, layer, [(cc_ref, cbuf, csem), (rc_ref, rbuf, rsem)])
    scale = (MLA_NOPE + MLA_ROPE) ** -0.5

    @pl.when(c == 0)
    def _():
        _init_softmax(m_ref, l_ref, acc_ref)

    q = q_ref[0]
    ql = q[:, :MLA_KV_LORA]
    qr = q[:, MLA_KV_LORA:MLA_KV_LORA + MLA_ROPE]
    c_c = cbuf[slot].reshape(n_pc * PAGE_SIZE, MLA_KV_LORA).astype(BF16)
    s_r = jnp.concatenate([_dot(qr, rbuf[slot, i].astype(BF16)) for i in range(n_pc)], axis=1)
    s = (_dot_nt(ql, c_c) + s_r) * scale
    _softmax_step(s, c_c, m_ref, l_ref, acc_ref)

    @pl.when(c == pl.num_programs(1) - 1)
    def _():
        sn = (jnp.sum(ql.astype(F32) * cn_ref[0], axis=-1, keepdims=True)
              + jnp.sum(qr.astype(F32) * rn_ref[0], axis=-1, keepdims=True)) * scale
        o_lat = _finish_decode(sn, cn_ref[0], m_ref, l_ref, acc_ref).astype(BF16)
        row = lax.broadcasted_iota(jnp.int32, (MLA_HEADS, HEAD_DIM), 0)
        for h in range(MLA_HEADS):
            full = _dot(o_lat, wuv_ref[h])
            o_ref[0, :, h * HEAD_DIM:(h + 1) * HEAD_DIM] = jnp.sum(
                jnp.where(row == h, full, 0.0), axis=0, keepdims=True)


def dec_mla(pt, qp, c_new, r_new, wuv, c_cache, r_cache, layer, n_pc):
    s_n, n_pages = pt.shape
    n_c = n_pages // n_pc
    grid_spec = pltpu.PrefetchScalarGridSpec(
        num_scalar_prefetch=1,
        grid=(s_n, n_c),
        in_specs=[pl.BlockSpec((1, MLA_HEADS, 256), lambda s, c, pt: (s, 0, 0)),
                  pl.BlockSpec((1, 1, MLA_KV_LORA), lambda s, c, pt: (s, 0, 0)),
                  pl.BlockSpec((1, 1, MLA_ROPE), lambda s, c, pt: (s, 0, 0)),
                  pl.BlockSpec(wuv.shape, lambda s, c, pt: (0, 0, 0)),
                  pl.BlockSpec(memory_space=pl.ANY),
                  pl.BlockSpec(memory_space=pl.ANY)],
        out_specs=pl.BlockSpec((1, 1, MLA_HEADS * HEAD_DIM), lambda s, c, pt: (s, 0, 0)),
        scratch_shapes=[pltpu.VMEM((2, n_pc, PAGE_SIZE, MLA_KV_LORA), F32),
                        pltpu.VMEM((2, n_pc, MLA_ROPE, PAGE_SIZE), F32),
                        pltpu.SemaphoreType.DMA((2,)), pltpu.SemaphoreType.DMA((2,)),
                        pltpu.VMEM((MLA_HEADS, 1), F32), pltpu.VMEM((MLA_HEADS, 1), F32),
                        pltpu.VMEM((MLA_HEADS, MLA_KV_LORA), F32)])
    return pl.pallas_call(
        functools.partial(_dec_mla_kernel, layer=layer, n_pc=n_pc),
        grid_spec=grid_spec,
        out_shape=jax.ShapeDtypeStruct((s_n, 1, MLA_HEADS * HEAD_DIM), F32),
        compiler_params=_cparams(("arbitrary", "arbitrary")),
        name="dec_mla",
    )(pt, qp, c_new, r_new, wuv, c_cache, r_cache)


def _dec_logf_kernel(pt_ref, ln_ref, lc_ref, o_ref, lbuf, lsem, *, layer, n_pages):
    slot = _stream_step(pt_ref, n_pages, layer, [(lc_ref, lbuf, lsem)])
    n = n_pages * PAGE_SIZE
    xt = jnp.concatenate([lbuf[slot, i] for i in range(n_pages)], axis=1)
    lane = lax.broadcasted_iota(jnp.int32, xt.shape, 1)
    s = 1
    while s < n:
        xt = xt + jnp.where(lane >= s, pltpu.roll(xt, s, axis=1), 0.0)
        s *= 2
    total = xt[:, n - 1:n] + ln_ref[0]
    o_ref[0] = total - xt


def dec_logf(pt, lf_new, lf_cache, layer):
    s_n, n_pages = pt.shape
    n = n_pages * PAGE_SIZE
    grid_spec = pltpu.PrefetchScalarGridSpec(
        num_scalar_prefetch=1,
        grid=(s_n, 1),
        in_specs=[pl.BlockSpec((1, FOX_HEADS, 1), lambda s, c, pt: (s, 0, 0)),
                  pl.BlockSpec(memory_space=pl.ANY)],
        out_specs=pl.BlockSpec((1, FOX_HEADS, n), lambda s, c, pt: (s, 0, 0)),
        scratch_shapes=[pltpu.VMEM((2, n_pages, FOX_HEADS, PAGE_SIZE), F32),
                        pltpu.SemaphoreType.DMA((2,))])
    return pl.pallas_call(
        functools.partial(_dec_logf_kernel, layer=layer, n_pages=n_pages),
        grid_spec=grid_spec,
        out_shape=jax.ShapeDtypeStruct((s_n, FOX_HEADS, n), F32),
        compiler_params=_cparams(("arbitrary", "arbitrary")),
        name="dec_logf",
    )(pt, lf_new, lf_cache)


def _dec_moba_gate_kernel(pt_ref, q_ref, kc_ref, o_ref, kbuf, ksem, ksum_ref, *, layer, n_pc, n_blocks):
    c = pl.program_id(1)
    slot = _stream_step(pt_ref, n_pc, layer, [(kc_ref, kbuf, ksem)])
    bpc = n_pc * PAGE_SIZE // MOBA_BLOCK

    @pl.when(c == 0)
    def _():
        ksum_ref[...] = jnp.zeros_like(ksum_ref)

    sums = jnp.sum(kbuf[slot].reshape(bpc, MOBA_BLOCK, HEAD_DIM), axis=1)
    ksum_ref[pl.ds(pl.multiple_of(c * bpc, bpc), bpc), :] = sums

    @pl.when(c == pl.num_programs(1) - 1)
    def _():
        kmean = ksum_ref[...] * (1.0 / MOBA_BLOCK)
        gate = _dot_nt(q_ref[0], kmean, precision=lax.Precision.HIGHEST)
        lane = lax.broadcasted_iota(jnp.int32, gate.shape, 1)
        past = lane < n_blocks
        g = jnp.where(past, gate, NEG_INF)
        cnt = jnp.zeros(gate.shape, F32)
        for m in range(n_blocks):
            gm = g[:, m:m + 1]
            cnt = cnt + jnp.where((gm > g) | ((gm == g) & (lane > m)), 1.0, 0.0)
        lane_f = lane.astype(F32)
        out = jnp.zeros(gate.shape, F32)
        for j in range(MOBA_TOPK):
            idx = jnp.sum(jnp.where(past & (cnt == float(j)), lane_f, 0.0), axis=-1, keepdims=True)
            out = jnp.where(lane == j, idx, out)
        o_ref[0] = out.astype(jnp.int32)


def dec_moba_gate(pt, q, k_cache, layer, n_pc):
    s_n, n_pages = pt.shape
    n_blocks = n_pages * PAGE_SIZE // MOBA_BLOCK
    assert n_blocks <= LANES and n_blocks >= MOBA_TOPK and n_pc % 2 == 0
    grid_spec = pltpu.PrefetchScalarGridSpec(
        num_scalar_prefetch=1,
        grid=(s_n, n_pages // n_pc),
        in_specs=[pl.BlockSpec((1, 8, HEAD_DIM), lambda s, c, pt: (s, 0, 0)),
                  pl.BlockSpec(memory_space=pl.ANY)],
        out_specs=pl.BlockSpec((1, 8, LANES), lambda s, c, pt: (s, 0, 0)),
        scratch_shapes=[pltpu.VMEM((2, n_pc, PAGE_SIZE, HEAD_DIM), F32),
                        pltpu.SemaphoreType.DMA((2,)),
                        pltpu.VMEM((LANES, HEAD_DIM), F32)])
    return pl.pallas_call(
        functools.partial(_dec_moba_gate_kernel, layer=layer, n_pc=n_pc, n_blocks=n_blocks),
        grid_spec=grid_spec,
        out_shape=jax.ShapeDtypeStruct((s_n, 8, LANES), jnp.int32),
        compiler_params=_cparams(("arbitrary", "arbitrary")),
        name="dec_moba_gate",
    )(pt, q, k_cache)


def _dec_moba_att_kernel(pt_ref, sel_ref, q_ref, kn_ref, vn_ref, bias_ref, bown_ref, kc_ref, vc_ref,
                         o_ref, kbuf, vbuf, ksem, vsem, *, layer):
    n_sel = MOBA_HEADS * MOBA_TOPK
    ppb = MOBA_BLOCK // PAGE_SIZE
    s = pl.program_id(0)
    n_s = pl.num_programs(0)
    slot = lax.rem(s, 2)
    streams = [(kc_ref, kbuf, ksem), (vc_ref, vbuf, vsem)]

    def issue(seq, slt):
        for p in range(n_sel):
            blk = sel_ref[seq, p]
            for e in range(ppb):
                page = pt_ref[seq, blk * ppb + e]
                for cache_ref, buf_ref, sem_ref in streams:
                    _page_copy(cache_ref, layer, page, buf_ref, slt, p * ppb + e, sem_ref).start()

    @pl.when(s == 0)
    def _():
        issue(s, slot)

    @pl.when(s + 1 < n_s)
    def _():
        issue(s + 1, 1 - slot)

    _wait_chunk(slot, n_sel * ppb, layer, streams)

    scale = HEAD_DIM ** -0.5
    q = q_ref[0]
    row = lax.broadcasted_iota(jnp.int32, (8, MOBA_BLOCK), 0)
    sc = []
    for j in range(MOBA_TOPK):
        sj = jnp.zeros((8, MOBA_BLOCK), F32)
        for h in range(MOBA_HEADS):
            p = h * MOBA_TOPK + j
            k_p = kbuf[slot, p * ppb:(p + 1) * ppb].reshape(MOBA_BLOCK, HEAD_DIM).astype(BF16)
            s_p = _dot_nt(q, k_p) * scale + bias_ref[sel_ref[s, p]]
            sj = jnp.where(row == h, s_p, sj)
        sc.append(sj)
    s_own = jnp.sum(q.astype(F32) * kn_ref[0], axis=-1, keepdims=True) * scale + bown_ref[...]
    m = s_own
    for sj in sc:
        m = jnp.maximum(m, jnp.max(sj, axis=-1, keepdims=True))
    p_own = jnp.exp(s_own - m)
    l = p_own
    acc = p_own * vn_ref[0]
    for j in range(MOBA_TOPK):
        pj = jnp.exp(sc[j] - m)
        l = l + jnp.sum(pj, axis=-1, keepdims=True)
        for h in range(MOBA_HEADS):
            p = h * MOBA_TOPK + j
            v_p = vbuf[slot, p * ppb:(p + 1) * ppb].reshape(MOBA_BLOCK, HEAD_DIM).astype(BF16)
            acc = acc + _dot(jnp.where(row == h, pj, 0.0).astype(BF16), v_p)
    o_ref[0] = acc / l


def dec_moba_att(pt, sel, q, k_new, v_new, bias_blk, bias_own, k_cache, v_cache, layer):
    s_n = pt.shape[0]
    n_blocks = bias_blk.shape[0]
    n_buf = MOBA_HEADS * MOBA_TOPK * (MOBA_BLOCK // PAGE_SIZE)
    grid_spec = pltpu.PrefetchScalarGridSpec(
        num_scalar_prefetch=2,
        grid=(s_n,),
        in_specs=[pl.BlockSpec((1, 8, HEAD_DIM), lambda s, pt, sel: (s, 0, 0)),
                  pl.BlockSpec((1, 1, HEAD_DIM), lambda s, pt, sel: (s, 0, 0)),
                  pl.BlockSpec((1, 1, HEAD_DIM), lambda s, pt, sel: (s, 0, 0)),
                  pl.BlockSpec((n_blocks, 8, MOBA_BLOCK), lambda s, pt, sel: (0, 0, 0)),
                  pl.BlockSpec((8, 1), lambda s, pt, sel: (0, 0)),
                  pl.BlockSpec(memory_space=pl.ANY),
                  pl.BlockSpec(memory_space=pl.ANY)],
        out_specs=pl.BlockSpec((1, 8, HEAD_DIM), lambda s, pt, sel: (s, 0, 0)),
        scratch_shapes=[pltpu.VMEM((2, n_buf, PAGE_SIZE, HEAD_DIM), F32),
                        pltpu.VMEM((2, n_buf, PAGE_SIZE, HEAD_DIM), F32),
                        pltpu.SemaphoreType.DMA((2,)), pltpu.SemaphoreType.DMA((2,))])
    return pl.pallas_call(
        functools.partial(_dec_moba_att_kernel, layer=layer),
        grid_spec=grid_spec,
        out_shape=jax.ShapeDtypeStruct((s_n, 8, HEAD_DIM), F32),
        compiler_params=_cparams(("arbitrary",)),
        name="dec_moba_att",
    )(pt, sel, q, k_new, v_new, bias_blk, bias_own, k_cache, v_cache)


def _t5_bucket(dist):
    n = jnp.maximum(dist, 0)
    max_exact = N_BUCKETS // 2
    large = max_exact + (jnp.log(jnp.maximum(n, 1).astype(F32) / max_exact)
                         / math.log(MAX_DISTANCE / max_exact) * (N_BUCKETS - max_exact)).astype(jnp.int32)
    return jnp.where(n < max_exact, n, jnp.minimum(large, N_BUCKETS - 1))


def _bias_lookup(bias_tab, bucket):
    n_h = bias_tab.shape[0]
    out = jnp.zeros((n_h,) + bucket.shape, F32)
    for b in range(N_BUCKETS):
        out = jnp.where(bucket[None] == b, bias_tab[:, b].reshape((n_h,) + (1,) * bucket.ndim), out)
    return out


def _far_bucket_is_constant(min_dist):
    max_exact = N_BUCKETS // 2
    v = max_exact + math.log(min_dist / max_exact) / math.log(MAX_DISTANCE / max_exact) * (N_BUCKETS - max_exact)
    return v >= N_BUCKETS - 1 + 0.5


def _rope_tables(pos):
    half = MLA_ROPE // 2
    inv = ROPE_THETA ** (-jnp.arange(half, dtype=F32) / half)
    ang = pos.astype(F32)[:, None] * inv[None, :]
    cos, sin = jnp.cos(ang), jnp.sin(ang)
    cos128 = jnp.concatenate([cos, cos, cos, cos], axis=1)
    sin128 = jnp.concatenate([-sin, sin, -sin, sin], axis=1)
    return cos128, sin128


def _layer_params(l, p):
    w_in = p["w_in"][l]
    cs = np.cumsum([0, 512, 128, 128, 512, 128, 128, FOX_HEADS, MLA_Q_LORA, MLA_KV_LORA, MLA_ROPE])
    seg = [w_in[:, cs[i]:cs[i + 1]] for i in range(10)]
    mq, mk, mv, fq, fk, fv, fz, cqa, ckv, kr = seg
    pad = jnp.zeros((w_in.shape[0], IN_PAD - C_FZ - FOX_HEADS), w_in.dtype)
    w_in_r = jnp.concatenate([mq, mk, mv, fq, fk, fv, cqa, ckv, kr, fz, pad], axis=1).astype(BF16)
    d_q = MLA_NOPE + MLA_ROPE
    w_uq = p["w_mla_uq"][l].reshape(MLA_Q_LORA, MLA_HEADS, d_q)
    w_uq_r = jnp.concatenate([w_uq[:, :, :MLA_NOPE].reshape(MLA_Q_LORA, -1),
                              w_uq[:, :, MLA_NOPE:].reshape(MLA_Q_LORA, -1)], axis=1).astype(BF16)
    g_q = p["g_mla_q"][l]
    g_kr = p["g_mla_kr"][l]
    return {
        "g_mix": p["g_mix"][l], "w_in": w_in_r, "b_fox_f": p["b_fox_f"][l],
        "g_moba_q": p["g_moba_q"][l], "g_moba_k": p["g_moba_k"][l],
        "g_fox_q": p["g_fox_q"][l], "g_fox_k": p["g_fox_k"][l],
        "g_mla_qa": p["g_mla_qa"][l], "w_uq": w_uq_r,
        "g_qn": g_q[:MLA_NOPE], "g_qr128": jnp.concatenate([g_q[MLA_NOPE:], g_q[MLA_NOPE:]]),
        "g_mla_kv": p["g_mla_kv"][l],
        "g_kr128": jnp.concatenate([g_kr, jnp.zeros((LANES - MLA_ROPE,), g_kr.dtype)]),
        "w_uk": jnp.transpose(p["w_mla_uk"][l], (1, 2, 0)).astype(BF16),
        "w_uv": jnp.transpose(p["w_mla_uv"][l], (1, 0, 2)).astype(BF16),
        "w_o": p["w_o"][l].astype(BF16),
        "g_x": p["g_x"][l], "g_mem": p["g_mem"][l],
        "w_xq": p["w_xq"][l].astype(BF16),
        "w_xkv": jnp.concatenate([p["w_xk"][l], p["w_xv"][l]], axis=1).astype(BF16),
        "g_xq": p["g_xq"][l], "g_xk": p["g_xk"][l],
        "w_xo": p["w_xo"][l].astype(BF16),
        "g_ffn": p["g_ffn"][l],
        "w_gate": p["w_gate"][l].astype(BF16), "w_up": p["w_up"][l].astype(BF16),
        "w_down": p["w_down"][l].astype(BF16),
    }


def kernel(x_prompt, x_sample, cache_moba_k, cache_moba_v, cache_fox_k, cache_fox_v, cache_fox_logf, cache_mla_ckv, cache_mla_krope, cache_mem_k, cache_mem_v, page_table, mem_prompt, rel_bias, g_mix, w_in, b_fox_f, g_moba_q, g_moba_k, g_fox_q, g_fox_k, g_mla_qa, w_mla_uq, g_mla_q, g_mla_kv, g_mla_kr, w_mla_uk, w_mla_uv, w_o, g_x, g_mem, w_xq, w_xk, w_xv, g_xq, g_xk, w_xo, g_ffn, w_gate, w_up, w_down):
    b, t, d = x_prompt.shape
    s_n = x_sample.shape[0]
    assert x_sample.shape[1] == 1
    depth = w_in.shape[0]
    n_pages = page_table.shape[1]
    past_len = n_pages * PAGE_SIZE
    mem_len = mem_prompt.shape[1]
    n_p = b * t
    blk = MOBA_BLOCK
    n_pc = _pick_tile(n_pages, 32, 2)
    params = dict(w_in=w_in, w_mla_uq=w_mla_uq, g_mla_q=g_mla_q, g_mla_kr=g_mla_kr, g_mix=g_mix,
                  b_fox_f=b_fox_f, g_moba_q=g_moba_q, g_moba_k=g_moba_k, g_fox_q=g_fox_q,
                  g_fox_k=g_fox_k, g_mla_qa=g_mla_qa, g_mla_kv=g_mla_kv, w_mla_uk=w_mla_uk,
                  w_mla_uv=w_mla_uv, w_o=w_o, g_x=g_x, g_mem=g_mem, w_xq=w_xq, w_xk=w_xk, w_xv=w_xv,
                  g_xq=g_xq, g_xk=g_xk, w_xo=w_xo, g_ffn=g_ffn, w_gate=w_gate, w_up=w_up,
                  w_down=w_down)

    pos = jnp.concatenate([jnp.tile(jnp.arange(t, dtype=jnp.int32), b),
                           jnp.full((s_n,), past_len, jnp.int32)])
    cos128, sin128 = _rope_tables(pos)
    assert _far_bucket_is_constant(blk + 1)
    bias_tab = rel_bias.T
    ii = jnp.arange(blk)[:, None] - jnp.arange(blk)[None, :]
    rows = MOBA_HEADS * blk
    bdiag = _bias_lookup(bias_tab, _t5_bucket(ii)).reshape(rows, blk)
    bprev = _bias_lookup(bias_tab, _t5_bucket(ii + blk)).reshape(rows, blk)
    bfar = jnp.repeat(bias_tab[:, N_BUCKETS - 1], blk).reshape(rows, 1)
    n_sblk = past_len // blk
    kpos = jnp.arange(past_len).reshape(n_sblk, blk)
    bias_blk = jnp.transpose(_bias_lookup(bias_tab, _t5_bucket(past_len - kpos)), (1, 0, 2))
    bias_blk = jnp.concatenate([bias_blk, jnp.zeros_like(bias_blk)], axis=1)
    bias_own = jnp.concatenate([bias_tab[:, 0], jnp.zeros((8 - MOBA_HEADS,), F32)]).reshape(8, 1)

    cache_logf_t = jnp.swapaxes(cache_fox_logf, 2, 3)
    cache_krope_t = jnp.swapaxes(cache_mla_krope, 2, 3)

    h = jnp.concatenate([x_prompt.reshape(n_p, d), x_sample.reshape(s_n, d)], axis=0)
    mem_x = mem_prompt.reshape(b * mem_len, d)
    rows_p, rows_s, mem_ks, mem_vs = [], [], [], []
    pad_heads = lambda a: jnp.concatenate([a, jnp.zeros_like(a)], axis=1)

    for l in range(depth):
        lp = _layer_params(l, params)
        z = norm_matmul(h, lp["g_mix"], lp["w_in"])
        mq, mk, mv, fq, fk, fv, lf, qp, ckv, kr, krp = post_project(z, lp, cos128, sin128)
        new_rows = (mk, mv, fk, fv, lf, ckv, kr)
        rows_p.append(tuple(a[:n_p].reshape(b, t, -1) for a in new_rows))
        rows_s.append(tuple(a[n_p:].reshape(s_n, 1, -1) for a in new_rows))
        pr = lambda a: a[:n_p].reshape(b, t, -1)
        sa = lambda a: a[n_p:]

        o_moba_p = moba_prompt(pr(mq), pr(mk), pr(mv), bdiag, bprev, bfar)
        cum, cumt = fox_cum(pr(lf), blk)
        o_fox_p = fox_prompt(pr(fq), pr(fk), pr(fv), cum, cumt, blk)
        o_mla_p = mla_prompt(pr(qp), pr(ckv), pr(krp), lp["w_uv"], blk)

        mq_s = pad_heads(sa(mq).reshape(s_n, MOBA_HEADS, HEAD_DIM))
        sel = dec_moba_gate(page_table, mq_s, cache_moba_k, l, n_pc)
        sel12 = sel[:, :MOBA_HEADS, :MOBA_TOPK].reshape(s_n, MOBA_HEADS * MOBA_TOPK)
        o_moba_s = dec_moba_att(page_table, sel12, mq_s.astype(BF16), sa(mk).reshape(s_n, 1, -1),
                                sa(mv).reshape(s_n, 1, -1), bias_blk, bias_own,
                                cache_moba_k, cache_moba_v, l)[:, :MOBA_HEADS]
        fbias = dec_logf(page_table, sa(lf).reshape(s_n, FOX_HEADS, 1), cache_logf_t, l)
        o_fox_s = dec_fox(page_table, sa(fq).reshape(s_n, FOX_HEADS, HEAD_DIM),
                          sa(fk).reshape(s_n, 1, -1), sa(fv).reshape(s_n, 1, -1), fbias,
                          cache_fox_k, cache_fox_v, l, n_pc)
        o_mla_s = dec_mla(page_table, sa(qp).reshape(s_n, MLA_HEADS, 256),
                          sa(ckv).reshape(s_n, 1, -1), sa(kr).reshape(s_n, 1, -1), lp["w_uv"],
                          cache_mla_ckv, cache_krope_t, l, n_pc)

        o_moba = jnp.concatenate([o_moba_p.reshape(n_p, -1), o_moba_s.reshape(s_n, -1).astype(BF16)])
        o_fox = jnp.concatenate([o_fox_p.reshape(n_p, -1), o_fox_s.reshape(s_n, -1).astype(BF16)])
        o_mla = jnp.concatenate([o_mla_p.reshape(n_p, -1), o_mla_s.reshape(s_n, -1).astype(BF16)])
        w_o_l = lp["w_o"]
        h = matmul_res([o_moba, o_fox, o_mla], [w_o_l[:512], w_o_l[512:1024], w_o_l[1024:]], h)

        kv = norm_matmul(mem_x, lp["g_mem"], lp["w_xkv"], tm_target=512, tn_target=512)
        xw = X_HEADS * HEAD_DIM
        mem_k = head_rms(kv[:, :xw], lp["g_xk"], X_HEADS)
        mem_v = kv[:, xw:]
        mem_ks.append(mem_k.reshape(b, mem_len, X_HEADS, HEAD_DIM))
        mem_vs.append(mem_v.reshape(b, mem_len, X_HEADS, HEAD_DIM))
        xq = norm_matmul(h, lp["g_x"], lp["w_xq"], tn_target=512)
        ox_p = xattn(xq[:n_p].reshape(b, t, xw), lp["g_xq"], mem_k.reshape(b, mem_len, xw),
                     mem_v.reshape(b, mem_len, xw))
        ox_s = xattn_dec(xq[n_p:].reshape(s_n, 1, xw), lp["g_xq"], cache_mem_k, cache_mem_v, l)
        ox = jnp.concatenate([ox_p.reshape(n_p, xw), ox_s.reshape(s_n, xw)])
        h = matmul_res([ox], [lp["w_xo"]], h)

        h = ffn(h, lp["g_ffn"], lp["w_gate"], lp["w_up"], lp["w_down"])

    stk = lambda rows, i: jnp.stack([r[i] for r in rows])
    return (h[:n_p].reshape(b, t, d), h[n_p:].reshape(s_n, 1, d),
            stk(rows_p, 0), stk(rows_p, 1), stk(rows_p, 2), stk(rows_p, 3),
            stk(rows_p, 4), stk(rows_p, 5), stk(rows_p, 6),
            jnp.stack(mem_ks), jnp.stack(mem_vs),
            stk(rows_s, 0), stk(rows_s, 1), stk(rows_s, 2), stk(rows_s, 3),
            stk(rows_s, 4), stk(rows_s, 5), stk(rows_s, 6))
```
